```python
import jax
import jax.numpy as jnp
from jax import lax
import numpy as np

D_MODEL = 4096
BATCH = 8
SEQ = 2048
DEPTH = 1

HEAD_DIM = 128
MIX_WIDTH = D_MODEL
ATTN_WIDTH = MIX_WIDTH // 2
REC_WIDTH = MIX_WIDTH - ATTN_WIDTH
ATTN_HEADS = ATTN_WIDTH // HEAD_DIM
REC_HEADS = REC_WIDTH // HEAD_DIM
REC_KEY_DIM = 128
REC_KEY_WIDTH = REC_HEADS * REC_KEY_DIM
ROPE_THETA = 500000.0
ROPE_DIM = HEAD_DIM // 4
MOBA_BLOCK = 256
MOBA_TOPK = 3
MOBA_QCHUNK = 16
REC_CHUNK = 64
NORM_EPS = 1e-6
NEG_INF = -1e30
SPLITS = (ATTN_WIDTH, ATTN_WIDTH, ATTN_WIDTH, ATTN_WIDTH, REC_KEY_WIDTH, REC_KEY_WIDTH, REC_WIDTH, REC_WIDTH)
SPLIT_POINTS = tuple(int(s) for s in np.cumsum(SPLITS)[:-1])
IN_WIDTH = sum(SPLITS)

kernel_name = 'hybrid_moba_hgrn2_layer'


def rms_norm(x, w):
    xf = x.astype(jnp.float32)
    y = xf * lax.rsqrt(jnp.mean(xf * xf, axis=-1, keepdims=True) + NORM_EPS)
    return (y * w.astype(jnp.float32)).astype(x.dtype)


def to_heads(a, n_heads):
    b, t, _ = a.shape
    return a.reshape(b, t, n_heads, -1).transpose(0, 2, 1, 3)


def from_heads(a):
    b, h, t, d = a.shape
    return a.transpose(0, 2, 1, 3).reshape(b, t, h * d)


def partial_rope(x, pos):
    half = ROPE_DIM // 2
    inv_freq = jnp.power(ROPE_THETA, -jnp.arange(half, dtype=jnp.float32) / half)
    ang = pos.astype(jnp.float32)[:, None] * inv_freq[None, :]
    cos = jnp.cos(ang).astype(x.dtype)
    sin = jnp.sin(ang).astype(x.dtype)
    x1 = x[..., :half]
    x2 = x[..., half:ROPE_DIM]
    return jnp.concatenate([x1 * cos - x2 * sin, x2 * cos + x1 * sin, x[..., ROPE_DIM:]], axis=-1)


def moba_attention(q, k, v):
    b, h, t, d = q.shape
    n_blocks = -(-t // MOBA_BLOCK)
    t_pad = n_blocks * MOBA_BLOCK
    pad = ((0, 0), (0, 0), (0, t_pad - t), (0, 0))
    q, k, v = jnp.pad(q, pad), jnp.pad(k, pad), jnp.pad(v, pad)
    k_blk = k.reshape(b, h, n_blocks, MOBA_BLOCK, d)
    v_blk = v.reshape(b, h, n_blocks, MOBA_BLOCK, d)
    k_mean = jnp.mean(k_blk.astype(jnp.float32), axis=3).astype(q.dtype)
    top_k = min(MOBA_TOPK, n_blocks)
    scale = d ** -0.5
    n_chunks = t_pad // MOBA_QCHUNK
    q_chunks = jnp.moveaxis(q.reshape(b, h, n_chunks, MOBA_QCHUNK, d), 2, 0)
    b_idx = jnp.arange(b)[:, None, None, None]
    h_idx = jnp.arange(h)[None, :, None, None]
    blk_ids = jnp.arange(n_blocks)
    sel_slots = jnp.arange(top_k)

    def chunk_step(args):
        c, q_c = args
        q_pos = c * MOBA_QCHUNK + jnp.arange(MOBA_QCHUNK)
        own = (c * MOBA_QCHUNK) // MOBA_BLOCK
        gate = jnp.einsum('bhqd,bhnd->bhqn', q_c, k_mean).astype(jnp.float32)
        gate = jnp.where(blk_ids < own, gate, NEG_INF)
        _, sel = lax.top_k(gate, top_k)
        k_sel = k_blk[b_idx, h_idx, sel]
        v_sel = v_blk[b_idx, h_idx, sel]
        s_past = jnp.einsum('bhqd,bhqnkd->bhqnk', q_c, k_sel).astype(jnp.float32) * scale
        s_past = jnp.where((sel_slots < own)[:, None], s_past, NEG_INF)
        k_own = lax.dynamic_index_in_dim(k_blk, own, axis=2, keepdims=False)
        v_own = lax.dynamic_index_in_dim(v_blk, own, axis=2, keepdims=False)
        s_own = jnp.einsum('bhqd,bhkd->bhqk', q_c, k_own).astype(jnp.float32) * scale
        k_pos = own * MOBA_BLOCK + jnp.arange(MOBA_BLOCK)
        s_own = jnp.where(k_pos[None, :] <= q_pos[:, None], s_own, NEG_INF)
        scores = jnp.concatenate([s_past.reshape(b, h, MOBA_QCHUNK, top_k * MOBA_BLOCK), s_own], axis=-1)
        p = jax.nn.softmax(scores, axis=-1).astype(v.dtype)
        p_past = p[..., :top_k * MOBA_BLOCK].reshape(b, h, MOBA_QCHUNK, top_k, MOBA_BLOCK)
        p_own = p[..., top_k * MOBA_BLOCK:]
        return (jnp.einsum('bhqnk,bhqnkd->bhqd', p_past, v_sel)
                + jnp.einsum('bhqk,bhkd->bhqd', p_own, v_own))

    out = lax.map(chunk_step, (jnp.arange(n_chunks), q_chunks))
    out = jnp.moveaxis(out, 0, 2).reshape(b, h, t_pad, d)
    return out[:, :, :t]


def hgrn2_recurrence(q, k, v, log_f):
    b, h, t, dk = q.shape
    dv = v.shape[-1]
    n = t // REC_CHUNK

    def chunks(a):
        return a.reshape(b, h, n, REC_CHUNK, a.shape[-1])

    q, k, v, log_f = chunks(q), chunks(k), chunks(v), chunks(log_f)
    cum = jnp.cumsum(log_f, axis=3)
    cum_last = cum[..., -1:, :]
    cum_mid = cum[..., REC_CHUNK // 2 - 1:REC_CHUNK // 2, :]
    q_in = q * jnp.exp(cum - cum_mid)
    k_in = k * jnp.exp(cum_mid - cum)
    scores = jnp.einsum('bhncd,bhnsd->bhncs', q_in, k_in)
    causal = jnp.arange(REC_CHUNK)[:, None] >= jnp.arange(REC_CHUNK)[None, :]
    scores = jnp.where(causal, scores, 0.0)
    o_intra = jnp.einsum('bhncs,bhnse->bhnce', scores, v)
    q_inter = q * jnp.exp(cum)
    k_state = k * jnp.exp(cum_last - cum)
    decay_chunk = jnp.exp(cum_last[..., 0, :])

    def scan_fn(state, xs):
        qi, ks, vc, dc = xs
        o = jnp.einsum('bhcd,bhde->bhce', qi, state)
        state = dc[..., :, None] * state + jnp.einsum('bhcd,bhce->bhde', ks, vc)
        return state, o

    xs = (jnp.moveaxis(q_inter, 2, 0), jnp.moveaxis(k_state, 2, 0),
          jnp.moveaxis(v, 2, 0), jnp.moveaxis(decay_chunk, 2, 0))
    state0 = jnp.zeros((b, h, dk, dv), jnp.float32)
    _, o_inter = lax.scan(scan_fn, state0, xs)
    o = o_intra + jnp.moveaxis(o_inter, 0, 2)
    return o.reshape(b, h, t, dv)


def setup_inputs(seed: int = 0) -> dict:
    key = jax.random.key(seed)
    ks = jax.random.split(key, 7)
    x = jax.random.normal(ks[0], (BATCH, SEQ, D_MODEL), jnp.float32)
    norm_w = 1.0 + 0.02 * jax.random.normal(ks[1], (DEPTH, D_MODEL), jnp.float32)
    w_in = jax.random.normal(ks[2], (DEPTH, D_MODEL, IN_WIDTH), jnp.float32) * D_MODEL ** -0.5
    rec_lower_bound_logits = 0.1 * jax.random.normal(ks[3], (DEPTH + 1, REC_KEY_WIDTH), jnp.float32)
    rec_out_norm_w = 1.0 + 0.02 * jax.random.normal(ks[4], (DEPTH, REC_WIDTH), jnp.float32)
    w_out = jax.random.normal(ks[5], (DEPTH, MIX_WIDTH, D_MODEL), jnp.float32) * MIX_WIDTH ** -0.5
    final_norm_w = 1.0 + 0.02 * jax.random.normal(ks[6], (D_MODEL,), jnp.float32)
    return {'x': x, 'norm_w': norm_w, 'w_in': w_in,
            'rec_lower_bound_logits': rec_lower_bound_logits,
            'rec_out_norm_w': rec_out_norm_w, 'w_out': w_out,
            'final_norm_w': final_norm_w}


def reference(x, norm_w, w_in, rec_lower_bound_logits, rec_out_norm_w, w_out, final_norm_w):
    t = x.shape[1]
    pos = jnp.arange(t)
    lb_all = jnp.cumsum(jax.nn.softmax(rec_lower_bound_logits.astype(jnp.float32), axis=0), axis=0)
    h = x
    for layer in range(DEPTH):
        u = rms_norm(h, norm_w[layer])
        proj = jnp.einsum('btd,de->bte', u, w_in[layer])
        a_q, a_k, a_v, a_g, r_q, r_f, r_i, r_g = jnp.split(proj, SPLIT_POINTS, axis=-1)

        q = partial_rope(to_heads(a_q, ATTN_HEADS), pos)
        k = partial_rope(to_heads(a_k, ATTN_HEADS), pos)
        attn = from_heads(moba_attention(q, k, to_heads(a_v, ATTN_HEADS)))
        attn = attn * jax.nn.silu(a_g)

        lb = lb_all[layer]
        z = r_f.astype(jnp.float32)
        log_f = jnp.log(lb + (1.0 - lb) * jax.nn.sigmoid(z))
        k_r = (1.0 - lb) * jax.nn.sigmoid(-z)
        q_r = jax.nn.silu(r_q.astype(jnp.float32)) * REC_KEY_DIM ** -0.5
        v_r = r_i.astype(jnp.float32)
        o_r = hgrn2_recurrence(to_heads(q_r, REC_HEADS), to_heads(k_r, REC_HEADS),
                               to_heads(v_r, REC_HEADS), to_heads(log_f, REC_HEADS))
        o_r = o_r * lax.rsqrt(jnp.mean(o_r * o_r, axis=-1, keepdims=True) + NORM_EPS)
        rec = (from_heads(o_r) * rec_out_norm_w[layer].astype(jnp.float32)).astype(h.dtype)
        rec = rec * jax.nn.silu(r_g)

        mixed = jnp.concatenate([attn, rec], axis=-1)
        h = h + jnp.einsum('bte,ed->btd', mixed, w_out[layer])
    return rms_norm(h, final_norm_w)
```

```python
import functools

import jax
import jax.numpy as jnp
import numpy as np
from jax import lax
from jax.experimental import pallas as pl
from jax.experimental.pallas import tpu as pltpu

HEAD_DIM = 128
ROPE_THETA = 500000.0
ROPE_DIM = HEAD_DIM // 4
MOBA_BLOCK = 256
MOBA_TOPK = 3
REC_CHUNK = 64
NORM_EPS = 1e-6
NEG_INF = -1e30

VMEM_LIMIT_BYTES = 56 * 1024 * 1024

F32 = jnp.float32
BF16 = jnp.bfloat16


def _params(n_grid_dims):
    return pltpu.CompilerParams(
        dimension_semantics=("arbitrary",) * n_grid_dims,
        vmem_limit_bytes=VMEM_LIMIT_BYTES,
    )


def _pick_block(dim, target, align):
    best = None
    for b in range(align, min(dim, target) + 1, align):
        if dim % b == 0:
            best = b
    assert best is not None, (dim, target, align)
    return best


def _sigmoid(x):
    a = jnp.exp(-jnp.abs(x))
    r = 1.0 / (1.0 + a)
    return jnp.where(x >= 0, r, a * r)


def _silu(x):
    return x * _sigmoid(x)


def _rmsnorm_kernel(x_ref, w_ref, o_ref):
    x = x_ref[...].astype(F32)
    ms = jnp.mean(x * x, axis=-1, keepdims=True)
    o_ref[...] = (x * lax.rsqrt(ms + NORM_EPS) * w_ref[...]).astype(o_ref.dtype)


def _rmsnorm(x2d, w_row, *, block_rows=256):
    n, d = x2d.shape
    block_rows = _pick_block(n, block_rows, 8)
    return pl.pallas_call(
        _rmsnorm_kernel,
        grid=(n // block_rows,),
        in_specs=[
            pl.BlockSpec((block_rows, d), lambda i: (i, 0)),
            pl.BlockSpec((1, d), lambda i: (0, 0)),
        ],
        out_specs=pl.BlockSpec((block_rows, d), lambda i: (i, 0)),
        out_shape=jax.ShapeDtypeStruct((n, d), BF16),
        compiler_params=_params(1),
        name="rmsnorm_in",
    )(x2d, w_row)


def _matmul_kernel(a_ref, b_ref, o_ref):
    o_ref[...] = jnp.dot(a_ref[...], b_ref[...], preferred_element_type=F32).astype(o_ref.dtype)


def _matmul(a, b, out_dtype, *, bm=1024, bn=1024, name="matmul"):
    m, k = a.shape
    _, n = b.shape
    bm = _pick_block(m, bm, 8)
    bn = _pick_block(n, bn, 128)
    return pl.pallas_call(
        _matmul_kernel,
        grid=(m // bm, n // bn),
        in_specs=[
            pl.BlockSpec((bm, k), lambda i, j: (i, 0)),
            pl.BlockSpec((k, bn), lambda i, j: (0, j)),
        ],
        out_specs=pl.BlockSpec((bm, bn), lambda i, j: (i, j)),
        out_shape=jax.ShapeDtypeStruct((m, n), out_dtype),
        compiler_params=_params(2),
        name=name,
    )(a, b)


def _attn_kernel(q_ref, k_ref, v_ref, g_ref, cos_ref, sin_ref, o_ref, kaug_ref, vaug_ref):
    t = q_ref.shape[0]
    nb = t // MOBA_BLOCK
    nbp = max(16, -(-nb // 16) * 16)
    half = ROPE_DIM // 2
    cos = cos_ref[...]
    sin = sin_ref[...]
    lane = lax.broadcasted_iota(jnp.int32, (t, HEAD_DIM), 1)

    def rope(x):
        partner = jnp.where(lane < half, pltpu.roll(x, HEAD_DIM - half, 1), pltpu.roll(x, half, 1))
        return x * cos + partner * sin

    k = rope(k_ref[...].astype(F32))
    q = rope(q_ref[...].astype(F32))
    q_s = (q * (HEAD_DIM ** -0.5)).astype(BF16)

    blk_of_key = lax.broadcasted_iota(jnp.int32, (t, HEAD_DIM), 0) // MOBA_BLOCK
    kaug_ref[:, :HEAD_DIM] = k.astype(BF16)
    kaug_ref[:, HEAD_DIM:] = (blk_of_key == lane).astype(BF16)
    vaug_ref[:, :HEAD_DIM] = v_ref[...]
    vaug_ref[:, HEAD_DIM:] = jnp.ones((t, HEAD_DIM), BF16)

    k_mean = jnp.mean(k.reshape(nb, MOBA_BLOCK, HEAD_DIM), axis=1)
    k_mean = jnp.concatenate([k_mean, jnp.zeros((nbp - nb, HEAD_DIM), F32)], axis=0).astype(BF16)
    gate_t = lax.dot_general(k_mean, q_s, (((1,), (1,)), ((), ())), preferred_element_type=F32)

    row_blk = lax.broadcasted_iota(jnp.int32, (nbp, MOBA_BLOCK), 0)
    q_idx = lax.broadcasted_iota(jnp.int32, (MOBA_BLOCK, MOBA_BLOCK), 0)
    k_idx = lax.broadcasted_iota(jnp.int32, (MOBA_BLOCK, MOBA_BLOCK), 1)
    causal = k_idx <= q_idx

    for qb in range(nb):
        rows = slice(qb * MOBA_BLOCK, (qb + 1) * MOBA_BLOCK)
        nk = (qb + 1) * MOBA_BLOCK
        q_blk = q_s[rows]
        if qb > MOBA_TOPK:
            g = gate_t[:, rows]
            rank = jnp.zeros((nbp, MOBA_BLOCK), F32)
            for m in range(qb):
                gm = g[m:m + 1, :]
                beats = (gm > g) | ((gm == g) & (m < row_blk))
                rank = rank + beats.astype(F32)
            keep = ((rank < MOBA_TOPK) & (row_blk < qb)) | (row_blk == qb)
            bias_t = jnp.where(keep, 0.0, NEG_INF)
            bias_t = jnp.concatenate([bias_t, jnp.zeros((HEAD_DIM - nbp, MOBA_BLOCK), F32)], axis=0)
            q_aug = jnp.concatenate([q_blk, bias_t.T.astype(BF16)], axis=1)
            s = lax.dot_general(q_aug, kaug_ref[:nk, :], (((1,), (1,)), ((), ())),
                                preferred_element_type=F32)
        else:
            s = lax.dot_general(q_blk, kaug_ref[:nk, :HEAD_DIM], (((1,), (1,)), ((), ())),
                                preferred_element_type=F32)
        s_own = jnp.where(causal, s[:, qb * MOBA_BLOCK:], NEG_INF)
        if qb > 0:
            s = jnp.concatenate([s[:, :qb * MOBA_BLOCK], s_own], axis=1)
        else:
            s = s_own
        m_row = jnp.max(s, axis=-1, keepdims=True)
        p = jnp.exp(s - m_row).astype(BF16)
        oa = jnp.dot(p, vaug_ref[:nk, :], preferred_element_type=F32)
        o = oa[:, :HEAD_DIM] / oa[:, HEAD_DIM:]
        o_ref[rows, :] = (o * _silu(g_ref[rows, :].astype(F32))).astype(o_ref.dtype)


def _moba_attention(proj, cos_t, sin_t, *, batch, seq, n_heads):
    n = batch * seq
    spec = lambda off: pl.BlockSpec((seq, HEAD_DIM), lambda b, h: (b, off + h))
    tbl = pl.BlockSpec((seq, HEAD_DIM), lambda b, h: (0, 0))
    return pl.pallas_call(
        _attn_kernel,
        grid=(batch, n_heads),
        in_specs=[spec(0), spec(n_heads), spec(2 * n_heads), spec(3 * n_heads), tbl, tbl],
        out_specs=pl.BlockSpec((seq, HEAD_DIM), lambda b, h: (b, h)),
        out_shape=jax.ShapeDtypeStruct((n, n_heads * HEAD_DIM), BF16),
        scratch_shapes=[
            pltpu.VMEM((seq, 2 * HEAD_DIM), BF16),
            pltpu.VMEM((seq, 2 * HEAD_DIM), BF16),
        ],
        compiler_params=_params(2),
        name="moba_attention",
    )(proj, proj, proj, proj, cos_t, sin_t)


def _rec_kernel(layer, q_ref, z_ref, v_ref, g_ref, lbl_ref, nw_ref, o_ref,
                qr_ref, kr_ref, cum_ref):
    t = q_ref.shape[0]
    n_chunks = t // REC_CHUNK

    logits = lbl_ref[...].astype(F32)
    e = jnp.exp(logits - jnp.max(logits, axis=0, keepdims=True))
    probs = e / jnp.sum(e, axis=0, keepdims=True)
    lb = jnp.sum(probs[:layer + 1], axis=0, keepdims=True)

    z = z_ref[...]
    a = jnp.exp(-jnp.abs(z))
    r = 1.0 / (1.0 + a)
    sig = jnp.where(z >= 0, r, a * r)
    nsig = jnp.where(z >= 0, a * r, r)
    log_f = jnp.log(lb + (1.0 - lb) * sig)
    kr_ref[...] = (1.0 - lb) * nsig
    qr_ref[...] = _silu(q_ref[...].astype(F32)) * (HEAD_DIM ** -0.5)

    pos = lax.broadcasted_iota(jnp.int32, (t, HEAD_DIM), 0) % REC_CHUNK
    cum = log_f
    shift = 1
    while shift < REC_CHUNK:
        cum = cum + jnp.where(pos >= shift, pltpu.roll(cum, shift, 0), 0.0)
        shift *= 2
    cum_ref[...] = cum

    ri = lax.broadcasted_iota(jnp.int32, (REC_CHUNK, REC_CHUNK), 0)
    ci = lax.broadcasted_iota(jnp.int32, (REC_CHUNK, REC_CHUNK), 1)
    tril = ri >= ci
    nw = nw_ref[...].astype(F32)
    mid = REC_CHUNK // 2 - 1

    def chunk(c, state_t):
        rows = pl.ds(pl.multiple_of(c * REC_CHUNK, REC_CHUNK), REC_CHUNK)
        cum_c = cum_ref[rows, :]
        c_mid = cum_c[mid:mid + 1, :]
        c_last = cum_c[REC_CHUNK - 1:REC_CHUNK, :]
        q_in = qr_ref[rows, :] * jnp.exp(cum_c - c_mid)
        k_in = kr_ref[rows, :] * jnp.exp(c_mid - cum_c)
        v_c = v_ref[rows, :]
        scores = lax.dot_general(q_in.astype(BF16), k_in.astype(BF16), (((1,), (1,)), ((), ())),
                                 preferred_element_type=F32)
        scores = jnp.where(tril, scores, 0.0)
        o = jnp.dot(scores.astype(BF16), v_c, preferred_element_type=F32)
        q_inter = q_in * jnp.exp(c_mid)
        o = o + lax.dot_general(q_inter.astype(BF16), state_t.astype(BF16), (((1,), (1,)), ((), ())),
                                preferred_element_type=F32)
        k_state = k_in * jnp.exp(c_last - c_mid)
        upd = lax.dot_general(v_c, k_state.astype(BF16), (((0,), (0,)), ((), ())),
                              preferred_element_type=F32)
        state_t = state_t * jnp.exp(c_last) + upd
        o = o * lax.rsqrt(jnp.mean(o * o, axis=-1, keepdims=True) + NORM_EPS)
        o = o * nw * _silu(g_ref[rows, :].astype(F32))
        o_ref[rows, :] = o.astype(o_ref.dtype)
        return state_t

    lax.fori_loop(0, n_chunks, chunk, jnp.zeros((HEAD_DIM, HEAD_DIM), F32))


def _hgrn2(proj, z, lbl, nw_row, *, layer, batch, seq, n_attn_heads, n_heads):
    n = batch * seq
    base = 4 * n_attn_heads
    spec = lambda off: pl.BlockSpec((seq, HEAD_DIM), lambda b, h: (b, off + h))
    return pl.pallas_call(
        functools.partial(_rec_kernel, layer),
        grid=(batch, n_heads),
        in_specs=[
            spec(base),
            pl.BlockSpec((seq, HEAD_DIM), lambda b, h: (b, h)),
            spec(base + n_heads),
            spec(base + 2 * n_heads),
            pl.BlockSpec((lbl.shape[0], HEAD_DIM), lambda b, h: (0, h)),
            pl.BlockSpec((1, HEAD_DIM), lambda b, h: (0, h)),
        ],
        out_specs=pl.BlockSpec((seq, HEAD_DIM), lambda b, h: (b, h)),
        out_shape=jax.ShapeDtypeStruct((n, n_heads * HEAD_DIM), BF16),
        scratch_shapes=[
            pltpu.VMEM((seq, HEAD_DIM), F32),
            pltpu.VMEM((seq, HEAD_DIM), F32),
            pltpu.VMEM((seq, HEAD_DIM), F32),
        ],
        compiler_params=_params(2),
        name="hgrn2_recurrence",
    )(proj, z, proj, proj, lbl, nw_row)


def _out_kernel(final_norm, n_attn_k, attn_ref, rec_ref, w_ref, x_ref, nw_ref, o_ref):
    kk = pl.program_id(1)

    @pl.when(kk == 0)
    def _():
        o_ref[...] = jnp.zeros_like(o_ref)

    @pl.when(kk < n_attn_k)
    def _():
        o_ref[...] += jnp.dot(attn_ref[...], w_ref[...], preferred_element_type=F32)

    @pl.when(kk >= n_attn_k)
    def _():
        o_ref[...] += jnp.dot(rec_ref[...], w_ref[...], preferred_element_type=F32)

    @pl.when(kk == pl.num_programs(1) - 1)
    def _():
        h = x_ref[...] + o_ref[...]
        if final_norm:
            ms = jnp.mean(h * h, axis=-1, keepdims=True)
            h = h * lax.rsqrt(ms + NORM_EPS) * nw_ref[...]
        o_ref[...] = h


def _out_proj(attn, rec, w_bf16, x2d, nw_row, *, final_norm, bm=512, bk=512):
    n, d = x2d.shape
    a_w = attn.shape[1]
    r_w = rec.shape[1]
    bm = _pick_block(n, bm, 8)
    bk = _pick_block(int(np.gcd(a_w, r_w)), bk, 128)
    n_attn_k = a_w // bk
    n_k = (a_w + r_w) // bk
    return pl.pallas_call(
        functools.partial(_out_kernel, final_norm, n_attn_k),
        grid=(n // bm, n_k),
        in_specs=[
            pl.BlockSpec((bm, bk), lambda i, k: (i, jnp.minimum(k, n_attn_k - 1))),
            pl.BlockSpec((bm, bk), lambda i, k: (i, jnp.maximum(k - n_attn_k, 0))),
            pl.BlockSpec((bk, d), lambda i, k: (k, 0)),
            pl.BlockSpec((bm, d), lambda i, k: (i, 0)),
            pl.BlockSpec((1, d), lambda i, k: (0, 0)),
        ],
        out_specs=pl.BlockSpec((bm, d), lambda i, k: (i, 0)),
        out_shape=jax.ShapeDtypeStruct((n, d), F32),
        compiler_params=_params(2),
        name="out_proj",
    )(attn, rec, w_bf16, x2d, nw_row)


def _rope_tables(seq):
    half = ROPE_DIM // 2
    inv_freq = jnp.power(ROPE_THETA, -jnp.arange(half, dtype=F32) / half)
    ang = jnp.arange(seq, dtype=F32)[:, None] * inv_freq[None, :]
    cos, sin = jnp.cos(ang), jnp.sin(ang)
    rest = HEAD_DIM - ROPE_DIM
    cos_t = jnp.concatenate([cos, cos, jnp.ones((seq, rest), F32)], axis=1)
    sin_t = jnp.concatenate([-sin, sin, jnp.zeros((seq, rest), F32)], axis=1)
    return cos_t, sin_t


def kernel(x, norm_w, w_in, rec_lower_bound_logits, rec_out_norm_w, w_out, final_norm_w):
    batch, seq, d_model = x.shape
    depth = norm_w.shape[0]
    mix = w_out.shape[1]
    attn_w = mix // 2
    rec_w = mix - attn_w
    n_ah = attn_w // HEAD_DIM
    n_rh = rec_w // HEAD_DIM
    key_w = n_rh * HEAD_DIM
    f_lo = 4 * attn_w + key_w
    f_hi = f_lo + key_w
    assert seq % MOBA_BLOCK == 0 and seq % REC_CHUNK == 0
    assert w_in.shape[2] == f_hi + 2 * rec_w

    cos_t, sin_t = _rope_tables(seq)
    h = x.reshape(batch * seq, d_model)
    for layer in range(depth):
        w = w_in[layer]
        w_main = jnp.concatenate([w[:, :f_lo], w[:, f_hi:]], axis=1).astype(BF16)
        w_f = w[:, f_lo:f_hi].astype(BF16)
        u = _rmsnorm(h, norm_w[layer].reshape(1, d_model))
        proj = _matmul(u, w_main, BF16, name="in_proj_main")
        z = _matmul(u, w_f, F32, name="in_proj_forget")
        attn = _moba_attention(proj, cos_t, sin_t, batch=batch, seq=seq, n_heads=n_ah)
        rec = _hgrn2(proj, z, rec_lower_bound_logits, rec_out_norm_w[layer].reshape(1, rec_w),
                     layer=layer, batch=batch, seq=seq, n_attn_heads=n_ah, n_heads=n_rh)
        last = layer == depth - 1
        h = _out_proj(attn, rec, w_out[layer].astype(BF16), h, final_norm_w.reshape(1, d_model),
                      final_norm=last)
    return h.reshape(batch, seq, d_model)
```

```python
import functools

import jax
import jax.numpy as jnp
import numpy as np
from jax import lax
from jax.experimental import pallas as pl
from jax.experimental.pallas import tpu as pltpu

HEAD_DIM = 128
ROPE_THETA = 500000.0
ROPE_DIM = HEAD_DIM // 4
MOBA_BLOCK = 256
MOBA_TOPK = 3
REC_CHUNK = 64
REC_GROUP = 8
NORM_EPS = 1e-6
NEG_INF = -1e30

VMEM_LIMIT_BYTES = 56 * 1024 * 1024

F32 = jnp.float32
BF16 = jnp.bfloat16


def _params(n_grid_dims):
    return pltpu.CompilerParams(
        dimension_semantics=("arbitrary",) * n_grid_dims,
        vmem_limit_bytes=VMEM_LIMIT_BYTES,
    )


def _pick_block(dim, target, align):
    best = None
    for b in range(align, min(dim, target) + 1, align):
        if dim % b == 0:
            best = b
    assert best is not None, (dim, target, align)
    return best


def _sigmoid(x):
    a = jnp.exp(-jnp.abs(x))
    r = 1.0 / (1.0 + a)
    return jnp.where(x >= 0, r, a * r)


def _silu(x):
    return x * _sigmoid(x)


def _rmsnorm_kernel(x_ref, w_ref, o_ref):
    x = x_ref[...].astype(F32)
    ms = jnp.mean(x * x, axis=-1, keepdims=True)
    o_ref[...] = (x * lax.rsqrt(ms + NORM_EPS) * w_ref[...]).astype(o_ref.dtype)


def _rmsnorm(x2d, w_row, *, block_rows=256):
    n, d = x2d.shape
    block_rows = _pick_block(n, block_rows, 8)
    return pl.pallas_call(
        _rmsnorm_kernel,
        grid=(n // block_rows,),
        in_specs=[
            pl.BlockSpec((block_rows, d), lambda i: (i, 0)),
            pl.BlockSpec((1, d), lambda i: (0, 0)),
        ],
        out_specs=pl.BlockSpec((block_rows, d), lambda i: (i, 0)),
        out_shape=jax.ShapeDtypeStruct((n, d), BF16),
        compiler_params=_params(1),
        name="rmsnorm_in",
    )(x2d, w_row)


def _matmul_kernel(a_ref, b_ref, o_ref):
    o_ref[...] = jnp.dot(a_ref[...], b_ref[...], preferred_element_type=F32).astype(o_ref.dtype)


def _matmul_cols(a, b, out_dtype, *, col_lo, col_hi, skip_lo, skip_hi, bm=1024, bn=1024, name="matmul"):
    m, k = a.shape
    bm = _pick_block(m, bm, 8)
    bn = _pick_block(int(np.gcd.reduce([col_lo, col_hi, skip_lo, skip_hi, b.shape[1]])), bn, 128)
    t_lo, t_skip_lo, t_skip = col_lo // bn, skip_lo // bn, (skip_hi - skip_lo) // bn
    n = col_hi - col_lo - (skip_hi - skip_lo)

    def b_map(i, j):
        src = j + t_lo
        return (0, jnp.where(src >= t_skip_lo, src + t_skip, src))

    return pl.pallas_call(
        _matmul_kernel,
        grid=(m // bm, n // bn),
        in_specs=[
            pl.BlockSpec((bm, k), lambda i, j: (i, 0)),
            pl.BlockSpec((k, bn), b_map),
        ],
        out_specs=pl.BlockSpec((bm, bn), lambda i, j: (i, j)),
        out_shape=jax.ShapeDtypeStruct((m, n), out_dtype),
        compiler_params=_params(2),
        name=name,
    )(a, b)


def _attn_kernel(q_ref, k_ref, v_ref, g_ref, cos_ref, sin_ref, o_ref, kaug_ref, vaug_ref):
    t = q_ref.shape[0]
    nb = t // MOBA_BLOCK
    nbp = max(16, -(-nb // 16) * 16)
    half = ROPE_DIM // 2
    cos = cos_ref[...]
    sin = sin_ref[...]
    lane = lax.broadcasted_iota(jnp.int32, (t, HEAD_DIM), 1)

    def rope(x):
        partner = jnp.where(lane < half, pltpu.roll(x, HEAD_DIM - half, 1), pltpu.roll(x, half, 1))
        return x * cos + partner * sin

    k = rope(k_ref[...].astype(F32))
    q = rope(q_ref[...].astype(F32))
    q_s = (q * float(HEAD_DIM ** -0.5 * np.log2(np.e))).astype(BF16)

    blk_of_key = lax.broadcasted_iota(jnp.int32, (t, HEAD_DIM), 0) // MOBA_BLOCK
    kaug_ref[:, :HEAD_DIM] = k.astype(BF16)
    kaug_ref[:, HEAD_DIM:] = (blk_of_key == lane).astype(BF16)
    vaug_ref[:, :HEAD_DIM] = v_ref[...]
    vaug_ref[:, HEAD_DIM:] = jnp.ones((t, HEAD_DIM), BF16)

    k_mean = jnp.mean(k.reshape(nb, MOBA_BLOCK, HEAD_DIM), axis=1)
    k_mean = jnp.concatenate([k_mean, jnp.zeros((nbp - nb, HEAD_DIM), F32)], axis=0).astype(BF16)
    gate_t = lax.dot_general(k_mean, q_s, (((1,), (1,)), ((), ())), preferred_element_type=F32)

    row_blk = lax.broadcasted_iota(jnp.int32, (nbp, MOBA_BLOCK), 0)
    q_idx = lax.broadcasted_iota(jnp.int32, (MOBA_BLOCK, MOBA_BLOCK), 0)
    k_idx = lax.broadcasted_iota(jnp.int32, (MOBA_BLOCK, MOBA_BLOCK), 1)
    causal = k_idx <= q_idx

    def masked_scores(qb):
        rows = slice(qb * MOBA_BLOCK, (qb + 1) * MOBA_BLOCK)
        nk = (qb + 1) * MOBA_BLOCK
        q_blk = q_s[rows]
        if qb > MOBA_TOPK:
            g = gate_t[:, rows]
            rank = jnp.zeros((nbp, MOBA_BLOCK), F32)
            for m in range(qb):
                gm = g[m:m + 1, :]
                beats = (gm > g) | ((gm == g) & (m < row_blk))
                rank = rank + beats.astype(F32)
            keep = ((rank < MOBA_TOPK) & (row_blk < qb)) | (row_blk == qb)
            bias_t = jnp.where(keep, 0.0, NEG_INF)
            bias_t = jnp.concatenate([bias_t, jnp.zeros((HEAD_DIM - nbp, MOBA_BLOCK), F32)], axis=0)
            q_aug = jnp.concatenate([q_blk, bias_t.T.astype(BF16)], axis=1)
            s = lax.dot_general(q_aug, kaug_ref[:nk, :], (((1,), (1,)), ((), ())),
                                preferred_element_type=F32)
        else:
            s = lax.dot_general(q_blk, kaug_ref[:nk, :HEAD_DIM], (((1,), (1,)), ((), ())),
                                preferred_element_type=F32)
        s_own = jnp.where(causal, s[:, qb * MOBA_BLOCK:], NEG_INF)
        if qb > 0:
            return jnp.concatenate([s[:, :qb * MOBA_BLOCK], s_own], axis=1)
        return s_own

    s_next = masked_scores(0)
    for qb in range(nb):
        rows = slice(qb * MOBA_BLOCK, (qb + 1) * MOBA_BLOCK)
        nk = (qb + 1) * MOBA_BLOCK
        s = s_next
        if qb + 1 < nb:
            s_next = masked_scores(qb + 1)
        m_row = jnp.max(s, axis=-1, keepdims=True)
        p = jnp.exp2(s - m_row).astype(BF16)
        oa = jnp.dot(p, vaug_ref[:nk, :], preferred_element_type=F32)
        o = oa[:, :HEAD_DIM] / oa[:, HEAD_DIM:]
        o_ref[rows, :] = (o * _silu(g_ref[rows, :].astype(F32))).astype(o_ref.dtype)


def _moba_attention(proj, cos_t, sin_t, *, batch, seq, n_heads):
    n = batch * seq
    spec = lambda off: pl.BlockSpec((seq, HEAD_DIM), lambda b, h: (b, off + h))
    tbl = pl.BlockSpec((seq, HEAD_DIM), lambda b, h: (0, 0))
    return pl.pallas_call(
        _attn_kernel,
        grid=(batch, n_heads),
        in_specs=[spec(0), spec(n_heads), spec(2 * n_heads), spec(3 * n_heads), tbl, tbl],
        out_specs=pl.BlockSpec((seq, HEAD_DIM), lambda b, h: (b, h)),
        out_shape=jax.ShapeDtypeStruct((n, n_heads * HEAD_DIM), BF16),
        scratch_shapes=[
            pltpu.VMEM((seq, 2 * HEAD_DIM), BF16),
            pltpu.VMEM((seq, 2 * HEAD_DIM), BF16),
        ],
        compiler_params=_params(2),
        name="moba_attention",
    )(proj, proj, proj, proj, cos_t, sin_t)


def _rec_kernel(layer, q_ref, z_ref, v_ref, g_ref, lbl_ref, nw_ref, o_ref,
                qin_ref, rhs_ref, upd_ref, cum_ref):
    t = q_ref.shape[0]
    n_chunks = t // REC_CHUNK

    logits = lbl_ref[...].astype(F32)
    e = jnp.exp(logits - jnp.max(logits, axis=0, keepdims=True))
    probs = e / jnp.sum(e, axis=0, keepdims=True)
    lb = jnp.sum(probs[:layer + 1], axis=0, keepdims=True)

    z = z_ref[...]
    a = jnp.exp(-jnp.abs(z))
    r = 1.0 / (1.0 + a)
    sig = jnp.where(z >= 0, r, a * r)
    nsig = jnp.where(z >= 0, a * r, r)
    log_f = jnp.log(lb + (1.0 - lb) * sig)
    k_r = (1.0 - lb) * nsig
    q_r = _silu(q_ref[...].astype(F32)) * (HEAD_DIM ** -0.5)

    pos = lax.broadcasted_iota(jnp.int32, (t, HEAD_DIM), 0) % REC_CHUNK
    cum = log_f
    shift = 1
    while shift < REC_CHUNK:
        cum = cum + jnp.where(pos >= shift, pltpu.roll(cum, shift, 0), 0.0)
        shift *= 2
    cum_ref[...] = cum

    mid = REC_CHUNK // 2 - 1
    cum3 = cum.reshape(n_chunks, REC_CHUNK, HEAD_DIM)
    rel = cum3 - cum3[:, mid:mid + 1, :]
    qin_ref[...] = (q_r.reshape(cum3.shape) * jnp.exp(rel)).reshape(t, HEAD_DIM).astype(BF16)
    rhs_ref[:, HEAD_DIM:, :] = (k_r.reshape(cum3.shape) * jnp.exp(-rel)).astype(BF16)

    def chunk_rows(c):
        return slice(c * REC_CHUNK, (c + 1) * REC_CHUNK)

    for c in range(n_chunks):
        upd_ref[c] = lax.dot_general(v_ref[chunk_rows(c), :], rhs_ref[c, HEAD_DIM:, :],
                                     (((0,), (0,)), ((), ())), preferred_element_type=F32)

    state_t = jnp.zeros((HEAD_DIM, HEAD_DIM), F32)
    for c in range(n_chunks):
        c_mid = cum_ref[c * REC_CHUNK + mid:c * REC_CHUNK + mid + 1, :]
        c_last = cum_ref[(c + 1) * REC_CHUNK - 1:(c + 1) * REC_CHUNK, :]
        rhs_ref[c, :HEAD_DIM, :] = (state_t * jnp.exp(c_mid)).astype(BF16)
        state_t = state_t * jnp.exp(c_last) + upd_ref[c] * jnp.exp(c_last - c_mid)

    ri = lax.broadcasted_iota(jnp.int32, (REC_CHUNK, REC_CHUNK), 0)
    ci = lax.broadcasted_iota(jnp.int32, (REC_CHUNK, REC_CHUNK), 1)
    tril = ri >= ci
    nw = nw_ref[...].astype(F32)
    for g0 in range(0, n_chunks, REC_GROUP):
        group = range(g0, min(g0 + REC_GROUP, n_chunks))
        both = [lax.dot_general(qin_ref[chunk_rows(c), :], rhs_ref[c], (((1,), (1,)), ((), ())),
                                preferred_element_type=F32) for c in group]
        for c, b in zip(group, both):
            rows = chunk_rows(c)
            scores = jnp.where(tril, b[:, HEAD_DIM:], 0.0).astype(BF16)
            o = b[:, :HEAD_DIM] + jnp.dot(scores, v_ref[rows, :], preferred_element_type=F32)
            o = o * lax.rsqrt(jnp.mean(o * o, axis=-1, keepdims=True) + NORM_EPS)
            o = o * nw * _silu(g_ref[rows, :].astype(F32))
            o_ref[rows, :] = o.astype(o_ref.dtype)


def _hgrn2(proj, z, lbl, nw_row, *, layer, batch, seq, n_attn_heads, n_heads):
    n = batch * seq
    base = 4 * n_attn_heads
    spec = lambda off: pl.BlockSpec((seq, HEAD_DIM), lambda b, h: (b, off + h))
    return pl.pallas_call(
        functools.partial(_rec_kernel, layer),
        grid=(batch, n_heads),
        in_specs=[
            spec(base),
            pl.BlockSpec((seq, HEAD_DIM), lambda b, h: (b, h)),
            spec(base + n_heads),
            spec(base + 2 * n_heads),
            pl.BlockSpec((lbl.shape[0], HEAD_DIM), lambda b, h: (0, h)),
            pl.BlockSpec((1, HEAD_DIM), lambda b, h: (0, h)),
        ],
        out_specs=pl.BlockSpec((seq, HEAD_DIM), lambda b, h: (b, h)),
        out_shape=jax.ShapeDtypeStruct((n, n_heads * HEAD_DIM), BF16),
        scratch_shapes=[
            pltpu.VMEM((seq, HEAD_DIM), BF16),
            pltpu.VMEM((seq // REC_CHUNK, HEAD_DIM + REC_CHUNK, HEAD_DIM), BF16),
            pltpu.VMEM((seq // REC_CHUNK, HEAD_DIM, HEAD_DIM), F32),
            pltpu.VMEM((seq, HEAD_DIM), F32),
        ],
        compiler_params=_params(2),
        name="hgrn2_recurrence",
    )(proj, z, proj, proj, lbl, nw_row)


def _out_kernel(final_norm, n_attn_k, n_k, attn_ref, rec_ref, w_ref, x_ref, nw_ref, o_ref):
    kk = pl.program_id(1)
    cw = x_ref.shape[1]

    def partial_product():
        lhs = jnp.where(kk < n_attn_k, attn_ref[...], rec_ref[...])
        return jnp.dot(lhs, w_ref[...], preferred_element_type=F32)

    def add_residual(j):
        o_ref[:, j * cw:(j + 1) * cw] += x_ref[...]

    def finish():
        if final_norm:
            h = o_ref[...]
            ms = jnp.mean(h * h, axis=-1, keepdims=True)
            o_ref[...] = h * lax.rsqrt(ms + NORM_EPS) * nw_ref[...]

    if n_k == 1:
        o_ref[...] = partial_product()
        add_residual(0)
        finish()
        return

    @pl.when(kk == 0)
    def _():
        o_ref[...] = partial_product()
        add_residual(0)

    @pl.when((kk > 0) & (kk < n_k - 1))
    def _():
        o_ref[...] += partial_product()

    for j in range(1, n_k - 1):
        @pl.when(kk == j)
        def _(j=j):
            add_residual(j)

    @pl.when(kk == n_k - 1)
    def _():
        o_ref[...] += partial_product()
        add_residual(n_k - 1)
        finish()


def _out_proj(attn, rec, w_bf16, x2d, nw_row, *, final_norm, bm=512, bk=1024):
    n, d = x2d.shape
    a_w = attn.shape[1]
    r_w = rec.shape[1]
    bm = _pick_block(n, bm, 8)
    bk = _pick_block(int(np.gcd(a_w, r_w)), bk, 128)
    n_attn_k = a_w // bk
    n_k = (a_w + r_w) // bk
    assert d % (n_k * 128) == 0
    return pl.pallas_call(
        functools.partial(_out_kernel, final_norm, n_attn_k, n_k),
        grid=(n // bm, n_k),
        in_specs=[
            pl.BlockSpec((bm, bk), lambda i, k: (i, jnp.minimum(k, n_attn_k - 1))),
            pl.BlockSpec((bm, bk), lambda i, k: (i, jnp.maximum(k - n_attn_k, 0))),
            pl.BlockSpec((bk, d), lambda i, k: (k, 0)),
            pl.BlockSpec((bm, d // n_k), lambda i, k: (i, k)),
            pl.BlockSpec((1, d), lambda i, k: (0, 0)),
        ],
        out_specs=pl.BlockSpec((bm, d), lambda i, k: (i, 0)),
        out_shape=jax.ShapeDtypeStruct((n, d), F32),
        compiler_params=_params(2),
        name="out_proj",
    )(attn, rec, w_bf16, x2d, nw_row)


def _rope_tables(seq):
    half = ROPE_DIM // 2
    inv_freq = jnp.power(ROPE_THETA, -jnp.arange(half, dtype=F32) / half)
    ang = jnp.arange(seq, dtype=F32)[:, None] * inv_freq[None, :]
    cos, sin = jnp.cos(ang), jnp.sin(ang)
    rest = HEAD_DIM - ROPE_DIM
    cos_t = jnp.concatenate([cos, cos, jnp.ones((seq, rest), F32)], axis=1)
    sin_t = jnp.concatenate([-sin, sin, jnp.zeros((seq, rest), F32)], axis=1)
    return cos_t, sin_t


def kernel(x, norm_w, w_in, rec_lower_bound_logits, rec_out_norm_w, w_out, final_norm_w):
    batch, seq, d_model = x.shape
    depth = norm_w.shape[0]
    mix = w_out.shape[1]
    attn_w = mix // 2
    rec_w = mix - attn_w
    n_ah = attn_w // HEAD_DIM
    n_rh = rec_w // HEAD_DIM
    key_w = n_rh * HEAD_DIM
    f_lo = 4 * attn_w + key_w
    f_hi = f_lo + key_w
    assert seq % MOBA_BLOCK == 0 and seq % REC_CHUNK == 0
    assert w_in.shape[2] == f_hi + 2 * rec_w

    cos_t, sin_t = _rope_tables(seq)
    h = x.reshape(batch * seq, d_model)
    for layer in range(depth):
        w = w_in[layer].astype(BF16)
        in_w = w.shape[1]
        u = _rmsnorm(h, norm_w[layer].reshape(1, d_model))
        proj = _matmul_cols(u, w, BF16, col_lo=0, col_hi=in_w, skip_lo=f_lo, skip_hi=f_hi,
                            name="in_proj_main")
        z = _matmul_cols(u, w, F32, col_lo=f_lo, col_hi=f_hi, skip_lo=in_w, skip_hi=in_w,
                         name="in_proj_forget")
        attn = _moba_attention(proj, cos_t, sin_t, batch=batch, seq=seq, n_heads=n_ah)
        rec = _hgrn2(proj, z, rec_lower_bound_logits, rec_out_norm_w[layer].reshape(1, rec_w),
                     layer=layer, batch=batch, seq=seq, n_attn_heads=n_ah, n_heads=n_rh)
        last = layer == depth - 1
        h = _out_proj(attn, rec, w_out[layer].astype(BF16), h, final_norm_w.reshape(1, d_model),
                      final_norm=last)
    return h.reshape(batch, seq, d_model)
```

```python
import functools

import jax
import jax.numpy as jnp
import numpy as np
from jax import lax
from jax.experimental import pallas as pl
from jax.experimental.pallas import tpu as pltpu

HEAD_DIM = 128
ROPE_THETA = 500000.0
ROPE_DIM = HEAD_DIM // 4
MOBA_BLOCK = 256
MOBA_TOPK = 3
REC_CHUNK = 64
REC_GROUP = 8
NORM_EPS = 1e-6
NEG_INF = -1e30

VMEM_LIMIT_BYTES = 60 * 1024 * 1024

F32 = jnp.float32
BF16 = jnp.bfloat16


def _params(n_grid_dims):
    return pltpu.CompilerParams(
        dimension_semantics=("arbitrary",) * n_grid_dims,
        vmem_limit_bytes=VMEM_LIMIT_BYTES,
    )


def _pick_block(dim, target, align):
    best = None
    for b in range(align, min(dim, target) + 1, align):
        if dim % b == 0:
            best = b
    assert best is not None, (dim, target, align)
    return best


def _sigmoid(x):
    a = jnp.exp(-jnp.abs(x))
    r = 1.0 / (1.0 + a)
    return jnp.where(x >= 0, r, a * r)


def _silu(x):
    return x * _sigmoid(x)


def _rmsnorm_kernel(x_ref, w_ref, o_ref):
    x = x_ref[...].astype(F32)
    ms = jnp.mean(x * x, axis=-1, keepdims=True)
    o_ref[...] = (x * lax.rsqrt(ms + NORM_EPS) * w_ref[...]).astype(o_ref.dtype)


def _rmsnorm(x2d, w_row, *, block_rows=256):
    n, d = x2d.shape
    block_rows = _pick_block(n, block_rows, 8)
    return pl.pallas_call(
        _rmsnorm_kernel,
        grid=(n // block_rows,),
        in_specs=[
            pl.BlockSpec((block_rows, d), lambda i: (i, 0)),
            pl.BlockSpec((1, d), lambda i: (0, 0)),
        ],
        out_specs=pl.BlockSpec((block_rows, d), lambda i: (i, 0)),
        out_shape=jax.ShapeDtypeStruct((n, d), BF16),
        compiler_params=_params(1),
        name="rmsnorm_in",
    )(x2d, w_row)


def _matmul_kernel(a_ref, b_ref, o_ref):
    o_ref[...] = jnp.dot(a_ref[...], b_ref[...].astype(BF16),
                         preferred_element_type=F32).astype(o_ref.dtype)


def _matmul_cols(a, w, layer, out_dtype, *, col_lo, col_hi, skip_lo, skip_hi, bm=1024, bn=512,
                 name="matmul"):
    m, k = a.shape
    bm = _pick_block(m, bm, 8)
    bn = _pick_block(int(np.gcd.reduce([col_lo, col_hi, skip_lo, skip_hi, w.shape[2]])), bn, 128)
    t_lo, t_skip_lo, t_skip = col_lo // bn, skip_lo // bn, (skip_hi - skip_lo) // bn
    n = col_hi - col_lo - (skip_hi - skip_lo)

    def b_map(i, j):
        src = j + t_lo
        return (layer, 0, jnp.where(src >= t_skip_lo, src + t_skip, src))

    return pl.pallas_call(
        _matmul_kernel,
        grid=(m // bm, n // bn),
        in_specs=[
            pl.BlockSpec((bm, k), lambda i, j: (i, 0)),
            pl.BlockSpec((None, k, bn), b_map),
        ],
        out_specs=pl.BlockSpec((bm, bn), lambda i, j: (i, j)),
        out_shape=jax.ShapeDtypeStruct((m, n), out_dtype),
        compiler_params=_params(2),
        name=name,
    )(a, w)


def _attn_kernel(q_ref, k_ref, v_ref, g_ref, cos_ref, sin_ref, o_ref, kaug_ref, vaug_ref):
    t = q_ref.shape[0]
    nb = t // MOBA_BLOCK
    nbp = max(16, -(-nb // 16) * 16)
    half = ROPE_DIM // 2
    cos = cos_ref[...]
    sin = sin_ref[...]
    lane = lax.broadcasted_iota(jnp.int32, (t, HEAD_DIM), 1)

    def rope(x):
        partner = jnp.where(lane < half, pltpu.roll(x, HEAD_DIM - half, 1), pltpu.roll(x, half, 1))
        return x * cos + partner * sin

    k = rope(k_ref[...].astype(F32))
    q = rope(q_ref[...].astype(F32))
    q_s = (q * float(HEAD_DIM ** -0.5 * np.log2(np.e))).astype(BF16)

    blk_of_key = lax.broadcasted_iota(jnp.int32, (t, HEAD_DIM), 0) // MOBA_BLOCK
    kaug_ref[:, :HEAD_DIM] = k.astype(BF16)
    kaug_ref[:, HEAD_DIM:] = (blk_of_key == lane).astype(BF16)
    vaug_ref[:, :HEAD_DIM] = v_ref[...]
    vaug_ref[:, HEAD_DIM:] = jnp.ones((t, HEAD_DIM), BF16)

    k_mean = jnp.mean(k.reshape(nb, MOBA_BLOCK, HEAD_DIM), axis=1)
    k_mean = jnp.concatenate([k_mean, jnp.zeros((nbp - nb, HEAD_DIM), F32)], axis=0).astype(BF16)
    gate_t = lax.dot_general(k_mean, q_s, (((1,), (1,)), ((), ())), preferred_element_type=F32)

    row_blk = lax.broadcasted_iota(jnp.int32, (nbp, MOBA_BLOCK), 0)
    q_idx = lax.broadcasted_iota(jnp.int32, (MOBA_BLOCK, MOBA_BLOCK), 0)
    k_idx = lax.broadcasted_iota(jnp.int32, (MOBA_BLOCK, MOBA_BLOCK), 1)
    causal = k_idx <= q_idx

    def masked_scores(qb):
        rows = slice(qb * MOBA_BLOCK, (qb + 1) * MOBA_BLOCK)
        nk = (qb + 1) * MOBA_BLOCK
        q_blk = q_s[rows]
        if qb > MOBA_TOPK:
            g = gate_t[:, rows]
            rank = jnp.zeros((nbp, MOBA_BLOCK), F32)
            for m in range(qb):
                gm = g[m:m + 1, :]
                beats = (gm > g) | ((gm == g) & (m < row_blk))
                rank = rank + beats.astype(F32)
            keep = ((rank < MOBA_TOPK) & (row_blk < qb)) | (row_blk == qb)
            bias_t = jnp.where(keep, 0.0, NEG_INF)
            bias_t = jnp.concatenate([bias_t, jnp.zeros((HEAD_DIM - nbp, MOBA_BLOCK), F32)], axis=0)
            q_aug = jnp.concatenate([q_blk, bias_t.T.astype(BF16)], axis=1)
            s = lax.dot_general(q_aug, kaug_ref[:nk, :], (((1,), (1,)), ((), ())),
                                preferred_element_type=F32)
        else:
            s = lax.dot_general(q_blk, kaug_ref[:nk, :HEAD_DIM], (((1,), (1,)), ((), ())),
                                preferred_element_type=F32)
        s_own = jnp.where(causal, s[:, qb * MOBA_BLOCK:], NEG_INF)
        if qb > 0:
            return jnp.concatenate([s[:, :qb * MOBA_BLOCK], s_own], axis=1)
        return s_own

    def finish(qb, p):
        rows = slice(qb * MOBA_BLOCK, (qb + 1) * MOBA_BLOCK)
        oa = jnp.dot(p, vaug_ref[:(qb + 1) * MOBA_BLOCK, :], preferred_element_type=F32)
        o = oa[:, :HEAD_DIM] / oa[:, HEAD_DIM:]
        o_ref[rows, :] = (o * _silu(g_ref[rows, :].astype(F32))).astype(o_ref.dtype)

    s_cur = masked_scores(0)
    p_prev = None
    for qb in range(nb):
        s_next = masked_scores(qb + 1) if qb + 1 < nb else None
        if p_prev is not None:
            finish(qb - 1, p_prev)
        m_row = jnp.max(s_cur, axis=-1, keepdims=True)
        p_prev = jnp.exp2(s_cur - m_row).astype(BF16)
        s_cur = s_next
    finish(nb - 1, p_prev)


def _moba_attention(proj, cos_t, sin_t, *, batch, seq, n_heads):
    n = batch * seq
    spec = lambda off: pl.BlockSpec((seq, HEAD_DIM), lambda b, h: (b, off + h))
    tbl = pl.BlockSpec((seq, HEAD_DIM), lambda b, h: (0, 0))
    return pl.pallas_call(
        _attn_kernel,
        grid=(batch, n_heads),
        in_specs=[spec(0), spec(n_heads), spec(2 * n_heads), spec(3 * n_heads), tbl, tbl],
        out_specs=pl.BlockSpec((seq, HEAD_DIM), lambda b, h: (b, h)),
        out_shape=jax.ShapeDtypeStruct((n, n_heads * HEAD_DIM), BF16),
        scratch_shapes=[
            pltpu.VMEM((seq, 2 * HEAD_DIM), BF16),
            pltpu.VMEM((seq, 2 * HEAD_DIM), BF16),
        ],
        compiler_params=_params(2),
        name="moba_attention",
    )(proj, proj, proj, proj, cos_t, sin_t)


def _rec_kernel(layer, q_ref, z_ref, v_ref, g_ref, lbl_ref, nw_ref, o_ref,
                qin_ref, rhs_ref, upd_ref, cum_ref):
    t = q_ref.shape[0]
    n_chunks = t // REC_CHUNK

    logits = lbl_ref[...].astype(F32)
    e = jnp.exp(logits - jnp.max(logits, axis=0, keepdims=True))
    probs = e / jnp.sum(e, axis=0, keepdims=True)
    lb = jnp.sum(probs[:layer + 1], axis=0, keepdims=True)

    z = z_ref[...]
    a = jnp.exp(-jnp.abs(z))
    r = 1.0 / (1.0 + a)
    sig = jnp.where(z >= 0, r, a * r)
    nsig = jnp.where(z >= 0, a * r, r)
    log_f = jnp.log(lb + (1.0 - lb) * sig)
    k_r = (1.0 - lb) * nsig
    q_r = _silu(q_ref[...].astype(F32)) * (HEAD_DIM ** -0.5)

    pos = lax.broadcasted_iota(jnp.int32, (t, HEAD_DIM), 0) % REC_CHUNK
    cum = log_f
    shift = 1
    while shift < REC_CHUNK:
        cum = cum + jnp.where(pos >= shift, pltpu.roll(cum, shift, 0), 0.0)
        shift *= 2
    cum_ref[...] = cum

    mid = REC_CHUNK // 2 - 1
    cum3 = cum.reshape(n_chunks, REC_CHUNK, HEAD_DIM)
    rel = cum3 - cum3[:, mid:mid + 1, :]
    qin_ref[...] = (q_r.reshape(cum3.shape) * jnp.exp(rel)).reshape(t, HEAD_DIM).astype(BF16)
    rhs_ref[:, HEAD_DIM:, :] = (k_r.reshape(cum3.shape) * jnp.exp(-rel)).astype(BF16)

    def chunk_rows(c):
        return slice(c * REC_CHUNK, (c + 1) * REC_CHUNK)

    for c in range(n_chunks):
        upd_ref[c] = lax.dot_general(v_ref[chunk_rows(c), :], rhs_ref[c, HEAD_DIM:, :],
                                     (((0,), (0,)), ((), ())), preferred_element_type=F32)

    state_t = jnp.zeros((HEAD_DIM, HEAD_DIM), F32)
    for c in range(n_chunks):
        c_mid = cum_ref[c * REC_CHUNK + mid:c * REC_CHUNK + mid + 1, :]
        c_last = cum_ref[(c + 1) * REC_CHUNK - 1:(c + 1) * REC_CHUNK, :]
        rhs_ref[c, :HEAD_DIM, :] = (state_t * jnp.exp(c_mid)).astype(BF16)
        state_t = state_t * jnp.exp(c_last) + upd_ref[c] * jnp.exp(c_last - c_mid)

    ri = lax.broadcasted_iota(jnp.int32, (REC_CHUNK, REC_CHUNK), 0)
    ci = lax.broadcasted_iota(jnp.int32, (REC_CHUNK, REC_CHUNK), 1)
    tril = ri >= ci
    nw = nw_ref[...].astype(F32)
    for g0 in range(0, n_chunks, REC_GROUP):
        group = range(g0, min(g0 + REC_GROUP, n_chunks))
        both = [lax.dot_general(qin_ref[chunk_rows(c), :], rhs_ref[c], (((1,), (1,)), ((), ())),
                                preferred_element_type=F32) for c in group]
        for c, b in zip(group, both):
            rows = chunk_rows(c)
            scores = jnp.where(tril, b[:, HEAD_DIM:], 0.0).astype(BF16)
            o = b[:, :HEAD_DIM] + jnp.dot(scores, v_ref[rows, :], preferred_element_type=F32)
            o = o * lax.rsqrt(jnp.mean(o * o, axis=-1, keepdims=True) + NORM_EPS)
            o = o * nw * _silu(g_ref[rows, :].astype(F32))
            o_ref[rows, :] = o.astype(o_ref.dtype)


def _hgrn2(proj, z, lbl, nw_row, *, layer, batch, seq, n_attn_heads, n_heads):
    n = batch * seq
    base = 4 * n_attn_heads
    spec = lambda off: pl.BlockSpec((seq, HEAD_DIM), lambda b, h: (b, off + h))
    return pl.pallas_call(
        functools.partial(_rec_kernel, layer),
        grid=(batch, n_heads),
        in_specs=[
            spec(base),
            pl.BlockSpec((seq, HEAD_DIM), lambda b, h: (b, h)),
            spec(base + n_heads),
            spec(base + 2 * n_heads),
            pl.BlockSpec((lbl.shape[0], HEAD_DIM), lambda b, h: (0, h)),
            pl.BlockSpec((1, HEAD_DIM), lambda b, h: (0, h)),
        ],
        out_specs=pl.BlockSpec((seq, HEAD_DIM), lambda b, h: (b, h)),
        out_shape=jax.ShapeDtypeStruct((n, n_heads * HEAD_DIM), BF16),
        scratch_shapes=[
            pltpu.VMEM((seq, HEAD_DIM), BF16),
            pltpu.VMEM((seq // REC_CHUNK, HEAD_DIM + REC_CHUNK, HEAD_DIM), BF16),
            pltpu.VMEM((seq // REC_CHUNK, HEAD_DIM, HEAD_DIM), F32),
            pltpu.VMEM((seq, HEAD_DIM), F32),
        ],
        compiler_params=_params(2),
        name="hgrn2_recurrence",
    )(proj, z, proj, proj, lbl, nw_row)


def _out_kernel(final_norm, n_attn_k, n_k, attn_ref, rec_ref, w_ref, x_ref, nw_ref, o_ref):
    kk = pl.program_id(1)
    cw = x_ref.shape[1]

    def partial_product():
        lhs = jnp.where(kk < n_attn_k, attn_ref[...], rec_ref[...])
        return jnp.dot(lhs, w_ref[...], preferred_element_type=F32)

    def add_residual(j):
        o_ref[:, j * cw:(j + 1) * cw] += x_ref[...]

    def finish():
        if final_norm:
            h = o_ref[...]
            ms = jnp.mean(h * h, axis=-1, keepdims=True)
            o_ref[...] = h * lax.rsqrt(ms + NORM_EPS) * nw_ref[...]

    if n_k == 1:
        o_ref[...] = partial_product()
        add_residual(0)
        finish()
        return

    @pl.when(kk == 0)
    def _():
        o_ref[...] = partial_product()
        add_residual(0)

    @pl.when((kk > 0) & (kk < n_k - 1))
    def _():
        o_ref[...] += partial_product()

    for j in range(1, n_k - 1):
        @pl.when(kk == j)
        def _(j=j):
            add_residual(j)

    @pl.when(kk == n_k - 1)
    def _():
        o_ref[...] += partial_product()
        add_residual(n_k - 1)
        finish()


def _out_proj(attn, rec, w_bf16, x2d, nw_row, *, final_norm, bm=1024, bk=512):
    n, d = x2d.shape
    a_w = attn.shape[1]
    r_w = rec.shape[1]
    bm = _pick_block(n, bm, 8)
    bk = _pick_block(int(np.gcd(a_w, r_w)), bk, 128)
    n_attn_k = a_w // bk
    n_k = (a_w + r_w) // bk
    assert d % (n_k * 128) == 0
    return pl.pallas_call(
        functools.partial(_out_kernel, final_norm, n_attn_k, n_k),
        grid=(n // bm, n_k),
        in_specs=[
            pl.BlockSpec((bm, bk), lambda i, k: (i, jnp.minimum(k, n_attn_k - 1))),
            pl.BlockSpec((bm, bk), lambda i, k: (i, jnp.maximum(k - n_attn_k, 0))),
            pl.BlockSpec((bk, d), lambda i, k: (k, 0)),
            pl.BlockSpec((bm, d // n_k), lambda i, k: (i, k)),
            pl.BlockSpec((1, d), lambda i, k: (0, 0)),
        ],
        out_specs=pl.BlockSpec((bm, d), lambda i, k: (i, 0)),
        out_shape=jax.ShapeDtypeStruct((n, d), F32),
        compiler_params=_params(2),
        name="out_proj",
    )(attn, rec, w_bf16, x2d, nw_row)


def _rope_tables(seq):
    half = ROPE_DIM // 2
    inv_freq = jnp.power(ROPE_THETA, -jnp.arange(half, dtype=F32) / half)
    ang = jnp.arange(seq, dtype=F32)[:, None] * inv_freq[None, :]
    cos, sin = jnp.cos(ang), jnp.sin(ang)
    rest = HEAD_DIM - ROPE_DIM
    cos_t = jnp.concatenate([cos, cos, jnp.ones((seq, rest), F32)], axis=1)
    sin_t = jnp.concatenate([-sin, sin, jnp.zeros((seq, rest), F32)], axis=1)
    return cos_t, sin_t


def kernel(x, norm_w, w_in, rec_lower_bound_logits, rec_out_norm_w, w_out, final_norm_w):
    batch, seq, d_model = x.shape
    depth = norm_w.shape[0]
    mix = w_out.shape[1]
    attn_w = mix // 2
    rec_w = mix - attn_w
    n_ah = attn_w // HEAD_DIM
    n_rh = rec_w // HEAD_DIM
    key_w = n_rh * HEAD_DIM
    f_lo = 4 * attn_w + key_w
    f_hi = f_lo + key_w
    assert seq % MOBA_BLOCK == 0 and seq % REC_CHUNK == 0
    assert w_in.shape[2] == f_hi + 2 * rec_w

    cos_t, sin_t = _rope_tables(seq)
    h = x.reshape(batch * seq, d_model)
    for layer in range(depth):
        in_w = w_in.shape[2]
        u = _rmsnorm(h, norm_w[layer].reshape(1, d_model))
        proj = _matmul_cols(u, w_in, layer, BF16, col_lo=0, col_hi=in_w, skip_lo=f_lo, skip_hi=f_hi,
                            name="in_proj_main")
        z = _matmul_cols(u, w_in, layer, F32, col_lo=f_lo, col_hi=f_hi, skip_lo=in_w, skip_hi=in_w,
                         name="in_proj_forget")
        attn = _moba_attention(proj, cos_t, sin_t, batch=batch, seq=seq, n_heads=n_ah)
        rec = _hgrn2(proj, z, rec_lower_bound_logits, rec_out_norm_w[layer].reshape(1, rec_w),
                     layer=layer, batch=batch, seq=seq, n_attn_heads=n_ah, n_heads=n_rh)
        last = layer == depth - 1
        h = _out_proj(attn, rec, w_out[layer].astype(BF16), h, final_norm_w.reshape(1, d_model),
                      final_norm=last)
    return h.reshape(batch, seq, d_model)
```

```python
import functools

import jax
import jax.numpy as jnp
import numpy as np
from jax import lax
from jax.experimental import pallas as pl
from jax.experimental.pallas import tpu as pltpu

HEAD_DIM = 128
ROPE_THETA = 500000.0
ROPE_DIM = HEAD_DIM // 4
MOBA_BLOCK = 256
MOBA_TOPK = 3
REC_CHUNK = 64
REC_GROUP = 8
NORM_EPS = 1e-6
NEG_INF = -1e30

VMEM_LIMIT_BYTES = 60 * 1024 * 1024

F32 = jnp.float32
BF16 = jnp.bfloat16


def _params(n_grid_dims):
    return pltpu.CompilerParams(
        dimension_semantics=("arbitrary",) * n_grid_dims,
        vmem_limit_bytes=VMEM_LIMIT_BYTES,
    )


def _pick_block(dim, target, align):
    best = None
    for b in range(align, min(dim, target) + 1, align):
        if dim % b == 0:
            best = b
    assert best is not None, (dim, target, align)
    return best


def _sigmoid(x):
    return 0.5 * jnp.tanh(0.5 * x) + 0.5


def _silu(x):
    return x * _sigmoid(x)


def _rmsnorm_kernel(x_ref, w_ref, o_ref):
    x = x_ref[...].astype(F32)
    ms = jnp.mean(x * x, axis=-1, keepdims=True)
    o_ref[...] = (x * lax.rsqrt(ms + NORM_EPS) * w_ref[...]).astype(o_ref.dtype)


def _rmsnorm(x2d, w_row, *, block_rows=256):
    n, d = x2d.shape
    block_rows = _pick_block(n, block_rows, 8)
    return pl.pallas_call(
        _rmsnorm_kernel,
        grid=(n // block_rows,),
        in_specs=[
            pl.BlockSpec((block_rows, d), lambda i: (i, 0)),
            pl.BlockSpec((1, d), lambda i: (0, 0)),
        ],
        out_specs=pl.BlockSpec((block_rows, d), lambda i: (i, 0)),
        out_shape=jax.ShapeDtypeStruct((n, d), BF16),
        compiler_params=_params(1),
        name="rmsnorm_in",
    )(x2d, w_row)


def _matmul_kernel(a_ref, b_ref, o_ref):
    o_ref[...] = jnp.dot(a_ref[...], b_ref[...], preferred_element_type=F32).astype(o_ref.dtype)


def _matmul_cols(a, b, out_dtype, *, col_lo, col_hi, skip_lo, skip_hi, bm=1024, bn=1024, name="matmul"):
    m, k = a.shape
    bm = _pick_block(m, bm, 8)
    bn = _pick_block(int(np.gcd.reduce([col_lo, col_hi, skip_lo, skip_hi, b.shape[1]])), bn, 128)
    t_lo, t_skip_lo, t_skip = col_lo // bn, skip_lo // bn, (skip_hi - skip_lo) // bn
    n = col_hi - col_lo - (skip_hi - skip_lo)

    def b_map(i, j):
        src = j + t_lo
        return (0, jnp.where(src >= t_skip_lo, src + t_skip, src))

    return pl.pallas_call(
        _matmul_kernel,
        grid=(m // bm, n // bn),
        in_specs=[
            pl.BlockSpec((bm, k), lambda i, j: (i, 0)),
            pl.BlockSpec((k, bn), b_map),
        ],
        out_specs=pl.BlockSpec((bm, bn), lambda i, j: (i, j)),
        out_shape=jax.ShapeDtypeStruct((m, n), out_dtype),
        compiler_params=_params(2),
        name=name,
    )(a, b)


def _attn_steps(q_ref, k_ref, v_ref, g_ref, cos_ref, sin_ref, o_ref, kaug_ref, vaug_ref):
    t = q_ref.shape[0]
    nb = t // MOBA_BLOCK
    nbp = max(16, -(-nb // 16) * 16)
    half = ROPE_DIM // 2
    cos = cos_ref[...]
    sin = sin_ref[...]
    lane = lax.broadcasted_iota(jnp.int32, (t, HEAD_DIM), 1)

    src = lax.broadcasted_iota(jnp.int32, (HEAD_DIM, HEAD_DIM), 0)
    dst = lax.broadcasted_iota(jnp.int32, (HEAD_DIM, HEAD_DIM), 1)
    pick = ((dst < half) & (src == dst + half)) | ((dst >= half) & (dst < 2 * half) & (src == dst - half))
    pick = pick.astype(BF16)

    def rope(x_bf16):
        partner = jnp.dot(x_bf16, pick, preferred_element_type=F32)
        return x_bf16.astype(F32) * cos + partner * sin

    k = rope(k_ref[...])
    q = rope(q_ref[...])
    q_s = (q * float(HEAD_DIM ** -0.5 * np.log2(np.e))).astype(BF16)

    blk_of_key = lax.broadcasted_iota(jnp.int32, (t, HEAD_DIM), 0) // MOBA_BLOCK
    kaug_ref[:, :HEAD_DIM] = k.astype(BF16)
    kaug_ref[:, HEAD_DIM:] = (blk_of_key == lane).astype(BF16)
    vaug_ref[:, :HEAD_DIM] = v_ref[...]
    vaug_ref[:, HEAD_DIM:] = jnp.ones((t, HEAD_DIM), BF16)

    k_mean = jnp.mean(k.reshape(nb, MOBA_BLOCK, HEAD_DIM), axis=1)
    k_mean = jnp.concatenate([k_mean, jnp.zeros((nbp - nb, HEAD_DIM), F32)], axis=0).astype(BF16)
    gate_t = lax.dot_general(k_mean, q_s, (((1,), (1,)), ((), ())), preferred_element_type=F32)

    row_blk = lax.broadcasted_iota(jnp.int32, (nbp, MOBA_BLOCK), 0)
    q_idx = lax.broadcasted_iota(jnp.int32, (MOBA_BLOCK, MOBA_BLOCK), 0)
    k_idx = lax.broadcasted_iota(jnp.int32, (MOBA_BLOCK, MOBA_BLOCK), 1)
    causal = k_idx <= q_idx

    def masked_scores(qb):
        rows = slice(qb * MOBA_BLOCK, (qb + 1) * MOBA_BLOCK)
        nk = (qb + 1) * MOBA_BLOCK
        q_blk = q_s[rows]
        if qb > MOBA_TOPK:
            g = gate_t[:, rows]
            rank = jnp.zeros((nbp, MOBA_BLOCK), F32)
            for m in range(qb):
                gm = g[m:m + 1, :]
                beats = (gm > g) | ((gm == g) & (m < row_blk))
                rank = rank + beats.astype(F32)
            keep = ((rank < MOBA_TOPK) & (row_blk < qb)) | (row_blk == qb)
            bias_t = jnp.where(keep, 0.0, NEG_INF)
            bias_t = jnp.concatenate([bias_t, jnp.zeros((HEAD_DIM - nbp, MOBA_BLOCK), F32)], axis=0)
            q_aug = jnp.concatenate([q_blk, bias_t.T.astype(BF16)], axis=1)
            s = lax.dot_general(q_aug, kaug_ref[:nk, :], (((1,), (1,)), ((), ())),
                                preferred_element_type=F32)
        else:
            s = lax.dot_general(q_blk, kaug_ref[:nk, :HEAD_DIM], (((1,), (1,)), ((), ())),
                                preferred_element_type=F32)
        s_own = jnp.where(causal, s[:, qb * MOBA_BLOCK:], NEG_INF)
        if qb > 0:
            return jnp.concatenate([s[:, :qb * MOBA_BLOCK], s_own], axis=1)
        return s_own

    def finish(qb, p):
        rows = slice(qb * MOBA_BLOCK, (qb + 1) * MOBA_BLOCK)
        oa = jnp.dot(p, vaug_ref[:(qb + 1) * MOBA_BLOCK, :], preferred_element_type=F32)
        o = oa[:, :HEAD_DIM] / oa[:, HEAD_DIM:]
        o_ref[rows, :] = (o * _silu(g_ref[rows, :].astype(F32))).astype(o_ref.dtype)

    s_cur = masked_scores(0)
    p_prev = None
    yield
    for qb in range(nb):
        s_next = masked_scores(qb + 1) if qb + 1 < nb else None
        if p_prev is not None:
            finish(qb - 1, p_prev)
        m_row = jnp.max(s_cur, axis=-1, keepdims=True)
        p_prev = jnp.exp2(s_cur - m_row).astype(BF16)
        s_cur = s_next
        yield
    finish(nb - 1, p_prev)
    yield


def _attn_kernel(*refs):
    for _ in _attn_steps(*refs):
        pass


def _moba_attention(proj, cos_t, sin_t, *, batch, seq, n_heads):
    n = batch * seq
    spec = lambda off: pl.BlockSpec((seq, HEAD_DIM), lambda b, h: (b, off + h))
    tbl = pl.BlockSpec((seq, HEAD_DIM), lambda b, h: (0, 0))
    return pl.pallas_call(
        _attn_kernel,
        grid=(batch, n_heads),
        in_specs=[spec(0), spec(n_heads), spec(2 * n_heads), spec(3 * n_heads), tbl, tbl],
        out_specs=pl.BlockSpec((seq, HEAD_DIM), lambda b, h: (b, h)),
        out_shape=jax.ShapeDtypeStruct((n, n_heads * HEAD_DIM), BF16),
        scratch_shapes=[
            pltpu.VMEM((seq, 2 * HEAD_DIM), BF16),
            pltpu.VMEM((seq, 2 * HEAD_DIM), BF16),
        ],
        compiler_params=_params(2),
        name="moba_attention",
    )(proj, proj, proj, proj, cos_t, sin_t)


def _rec_steps(layer, q_ref, z_ref, v_ref, g_ref, lbl_ref, nw_ref, o_ref,
               qin_ref, rhs_ref, upd_ref, cum_ref):
    t = q_ref.shape[0]
    n_chunks = t // REC_CHUNK

    logits = lbl_ref[...].astype(F32)
    e = jnp.exp(logits - jnp.max(logits, axis=0, keepdims=True))
    probs = e / jnp.sum(e, axis=0, keepdims=True)
    lb = jnp.sum(probs[:layer + 1], axis=0, keepdims=True)

    half_t = 0.5 * jnp.tanh(0.5 * z_ref[...])
    log2_f = jnp.log2(lb + (1.0 - lb) * (0.5 + half_t))
    k_r = (1.0 - lb) * (0.5 - half_t)
    q_r = _silu(q_ref[...].astype(F32)) * (HEAD_DIM ** -0.5)

    pos = lax.broadcasted_iota(jnp.int32, (t, HEAD_DIM), 0) % REC_CHUNK
    cum = log2_f
    shift = 1
    while shift < REC_CHUNK:
        cum = cum + jnp.where(pos >= shift, pltpu.roll(cum, shift, 0), 0.0)
        shift *= 2
    cum_ref[...] = cum

    mid = REC_CHUNK // 2 - 1
    cum3 = cum.reshape(n_chunks, REC_CHUNK, HEAD_DIM)
    rel = cum3 - cum3[:, mid:mid + 1, :]
    qin_ref[...] = (q_r.reshape(cum3.shape) * jnp.exp2(rel)).reshape(t, HEAD_DIM).astype(BF16)
    rhs_ref[:, HEAD_DIM:, :] = (k_r.reshape(cum3.shape) * jnp.exp2(-rel)).astype(BF16)
    yield

    def chunk_rows(c):
        return slice(c * REC_CHUNK, (c + 1) * REC_CHUNK)

    for c in range(n_chunks):
        upd_ref[c] = lax.dot_general(v_ref[chunk_rows(c), :], rhs_ref[c, HEAD_DIM:, :],
                                     (((0,), (0,)), ((), ())), preferred_element_type=F32)
        if (c + 1) % REC_GROUP == 0:
            yield

    state_t = jnp.zeros((HEAD_DIM, HEAD_DIM), F32)
    for c in range(n_chunks):
        c_mid = cum_ref[c * REC_CHUNK + mid:c * REC_CHUNK + mid + 1, :]
        c_last = cum_ref[(c + 1) * REC_CHUNK - 1:(c + 1) * REC_CHUNK, :]
        rhs_ref[c, :HEAD_DIM, :] = (state_t * jnp.exp2(c_mid)).astype(BF16)
        state_t = state_t * jnp.exp2(c_last) + upd_ref[c] * jnp.exp2(c_last - c_mid)
        if (c + 1) % REC_GROUP == 0:
            yield

    ri = lax.broadcasted_iota(jnp.int32, (REC_CHUNK, REC_CHUNK), 0)
    ci = lax.broadcasted_iota(jnp.int32, (REC_CHUNK, REC_CHUNK), 1)
    tril = ri >= ci
    nw = nw_ref[...].astype(F32)
    for g0 in range(0, n_chunks, REC_GROUP):
        group = range(g0, min(g0 + REC_GROUP, n_chunks))
        both = [lax.dot_general(qin_ref[chunk_rows(c), :], rhs_ref[c], (((1,), (1,)), ((), ())),
                                preferred_element_type=F32) for c in group]
        for c, b in zip(group, both):
            rows = chunk_rows(c)
            scores = jnp.where(tril, b[:, HEAD_DIM:], 0.0).astype(BF16)
            o = b[:, :HEAD_DIM] + jnp.dot(scores, v_ref[rows, :], preferred_element_type=F32)
            o = o * lax.rsqrt(jnp.mean(o * o, axis=-1, keepdims=True) + NORM_EPS)
            o = o * nw * _silu(g_ref[rows, :].astype(F32))
            o_ref[rows, :] = o.astype(o_ref.dtype)
        yield


def _rec_kernel(layer, *refs):
    for _ in _rec_steps(layer, *refs):
        pass


def _mixer_kernel(layer, aq, ak, av, ag, cos, sin, rq, rz, rv, rg, lbl, nw, attn_o, rec_o,
                  kaug, vaug, qin, rhs, upd, cum):
    streams = [_attn_steps(aq, ak, av, ag, cos, sin, attn_o, kaug, vaug),
               _rec_steps(layer, rq, rz, rv, rg, lbl, nw, rec_o, qin, rhs, upd, cum)]
    stop = object()
    while streams:
        streams = [g for g in streams if next(g, stop) is not stop]


def _mixers(proj, z, lbl, nw_row, cos_t, sin_t, *, layer, batch, seq, n_heads):
    n = batch * seq
    spec = lambda off: pl.BlockSpec((seq, HEAD_DIM), lambda b, h: (b, off + h))
    tbl = pl.BlockSpec((seq, HEAD_DIM), lambda b, h: (0, 0))
    n_chunks = seq // REC_CHUNK
    out = jax.ShapeDtypeStruct((n, n_heads * HEAD_DIM), BF16)
    return pl.pallas_call(
        functools.partial(_mixer_kernel, layer),
        grid=(batch, n_heads),
        in_specs=[
            spec(0), spec(n_heads), spec(2 * n_heads), spec(3 * n_heads), tbl, tbl,
            spec(4 * n_heads),
            pl.BlockSpec((seq, HEAD_DIM), lambda b, h: (b, h)),
            spec(5 * n_heads), spec(6 * n_heads),
            pl.BlockSpec((lbl.shape[0], HEAD_DIM), lambda b, h: (0, h)),
            pl.BlockSpec((1, HEAD_DIM), lambda b, h: (0, h)),
        ],
        out_specs=[pl.BlockSpec((seq, HEAD_DIM), lambda b, h: (b, h))] * 2,
        out_shape=[out, out],
        scratch_shapes=[
            pltpu.VMEM((seq, 2 * HEAD_DIM), BF16),
            pltpu.VMEM((seq, 2 * HEAD_DIM), BF16),
            pltpu.VMEM((seq, HEAD_DIM), BF16),
            pltpu.VMEM((n_chunks, HEAD_DIM + REC_CHUNK, HEAD_DIM), BF16),
            pltpu.VMEM((n_chunks, HEAD_DIM, HEAD_DIM), F32),
            pltpu.VMEM((seq, HEAD_DIM), F32),
        ],
        compiler_params=_params(2),
        name="mixers",
    )(proj, proj, proj, proj, cos_t, sin_t, proj, z, proj, proj, lbl, nw_row)


def _hgrn2(proj, z, lbl, nw_row, *, layer, batch, seq, n_attn_heads, n_heads):
    n = batch * seq
    base = 4 * n_attn_heads
    spec = lambda off: pl.BlockSpec((seq, HEAD_DIM), lambda b, h: (b, off + h))
    return pl.pallas_call(
        functools.partial(_rec_kernel, layer),
        grid=(batch, n_heads),
        in_specs=[
            spec(base),
            pl.BlockSpec((seq, HEAD_DIM), lambda b, h: (b, h)),
            spec(base + n_heads),
            spec(base + 2 * n_heads),
            pl.BlockSpec((lbl.shape[0], HEAD_DIM), lambda b, h: (0, h)),
            pl.BlockSpec((1, HEAD_DIM), lambda b, h: (0, h)),
        ],
        out_specs=pl.BlockSpec((seq, HEAD_DIM), lambda b, h: (b, h)),
        out_shape=jax.ShapeDtypeStruct((n, n_heads * HEAD_DIM), BF16),
        scratch_shapes=[
            pltpu.VMEM((seq, HEAD_DIM), BF16),
            pltpu.VMEM((seq // REC_CHUNK, HEAD_DIM + REC_CHUNK, HEAD_DIM), BF16),
            pltpu.VMEM((seq // REC_CHUNK, HEAD_DIM, HEAD_DIM), F32),
            pltpu.VMEM((seq, HEAD_DIM), F32),
        ],
        compiler_params=_params(2),
        name="hgrn2_recurrence",
    )(proj, z, proj, proj, lbl, nw_row)


def _out_kernel(final_norm, n_attn_k, n_k, attn_ref, rec_ref, w_ref, x_ref, nw_ref, o_ref):
    kk = pl.program_id(1)
    cw = x_ref.shape[1]

    def partial_product():
        lhs = jnp.where(kk < n_attn_k, attn_ref[...], rec_ref[...])
        return jnp.dot(lhs, w_ref[...], preferred_element_type=F32)

    def add_residual(j):
        o_ref[:, j * cw:(j + 1) * cw] += x_ref[...]

    def finish():
        if final_norm:
            h = o_ref[...]
            ms = jnp.mean(h * h, axis=-1, keepdims=True)
            o_ref[...] = h * lax.rsqrt(ms + NORM_EPS) * nw_ref[...]

    if n_k == 1:
        o_ref[...] = partial_product()
        add_residual(0)
        finish()
        return

    @pl.when(kk == 0)
    def _():
        o_ref[...] = partial_product()
        add_residual(0)

    @pl.when((kk > 0) & (kk < n_k - 1))
    def _():
        o_ref[...] += partial_product()

    for j in range(1, n_k - 1):
        @pl.when(kk == j)
        def _(j=j):
            add_residual(j)

    @pl.when(kk == n_k - 1)
    def _():
        o_ref[...] += partial_product()
        add_residual(n_k - 1)
        finish()


def _out_proj(attn, rec, w_bf16, x2d, nw_row, *, final_norm, bm=1024, bk=512):
    n, d = x2d.shape
    a_w = attn.shape[1]
    r_w = rec.shape[1]
    bm = _pick_block(n, bm, 8)
    bk = _pick_block(int(np.gcd(a_w, r_w)), bk, 128)
    n_attn_k = a_w // bk
    n_k = (a_w + r_w) // bk
    assert d % (n_k * 128) == 0
    return pl.pallas_call(
        functools.partial(_out_kernel, final_norm, n_attn_k, n_k),
        grid=(n // bm, n_k),
        in_specs=[
            pl.BlockSpec((bm, bk), lambda i, k: (i, jnp.minimum(k, n_attn_k - 1))),
            pl.BlockSpec((bm, bk), lambda i, k: (i, jnp.maximum(k - n_attn_k, 0))),
            pl.BlockSpec((bk, d), lambda i, k: (k, 0)),
            pl.BlockSpec((bm, d // n_k), lambda i, k: (i, k)),
            pl.BlockSpec((1, d), lambda i, k: (0, 0)),
        ],
        out_specs=pl.BlockSpec((bm, d), lambda i, k: (i, 0)),
        out_shape=jax.ShapeDtypeStruct((n, d), F32),
        compiler_params=_params(2),
        name="out_proj",
    )(attn, rec, w_bf16, x2d, nw_row)


def _rope_tables(seq):
    half = ROPE_DIM // 2
    inv_freq = jnp.power(ROPE_THETA, -jnp.arange(half, dtype=F32) / half)
    ang = jnp.arange(seq, dtype=F32)[:, None] * inv_freq[None, :]
    cos, sin = jnp.cos(ang), jnp.sin(ang)
    rest = HEAD_DIM - ROPE_DIM
    cos_t = jnp.concatenate([cos, cos, jnp.ones((seq, rest), F32)], axis=1)
    sin_t = jnp.concatenate([-sin, sin, jnp.zeros((seq, rest), F32)], axis=1)
    return cos_t, sin_t


def kernel(x, norm_w, w_in, rec_lower_bound_logits, rec_out_norm_w, w_out, final_norm_w):
    batch, seq, d_model = x.shape
    depth = norm_w.shape[0]
    mix = w_out.shape[1]
    attn_w = mix // 2
    rec_w = mix - attn_w
    n_ah = attn_w // HEAD_DIM
    n_rh = rec_w // HEAD_DIM
    key_w = n_rh * HEAD_DIM
    f_lo = 4 * attn_w + key_w
    f_hi = f_lo + key_w
    assert seq % MOBA_BLOCK == 0 and seq % REC_CHUNK == 0
    assert w_in.shape[2] == f_hi + 2 * rec_w

    cos_t, sin_t = _rope_tables(seq)
    h = x.reshape(batch * seq, d_model)
    for layer in range(depth):
        w = w_in[layer].astype(BF16)
        in_w = w.shape[1]
        u = _rmsnorm(h, norm_w[layer].reshape(1, d_model))
        proj = _matmul_cols(u, w, BF16, col_lo=0, col_hi=in_w, skip_lo=f_lo, skip_hi=f_hi,
                            name="in_proj_main")
        z = _matmul_cols(u, w, F32, col_lo=f_lo, col_hi=f_hi, skip_lo=in_w, skip_hi=in_w,
                         name="in_proj_forget")
        rec_nw = rec_out_norm_w[layer].reshape(1, rec_w)
        if n_ah == n_rh:
            attn, rec = _mixers(proj, z, rec_lower_bound_logits, rec_nw, cos_t, sin_t,
                                layer=layer, batch=batch, seq=seq, n_heads=n_ah)
        else:
            attn = _moba_attention(proj, cos_t, sin_t, batch=batch, seq=seq, n_heads=n_ah)
            rec = _hgrn2(proj, z, rec_lower_bound_logits, rec_nw,
                         layer=layer, batch=batch, seq=seq, n_attn_heads=n_ah, n_heads=n_rh)
        last = layer == depth - 1
        h = _out_proj(attn, rec, w_out[layer].astype(BF16), h, final_norm_w.reshape(1, d_model),
                      final_norm=last)
    return h.reshape(batch, seq, d_model)
```

```python
import functools

import jax
import jax.numpy as jnp
import numpy as np
from jax import lax
from jax.experimental import pallas as pl
from jax.experimental.pallas import tpu as pltpu

HEAD_DIM = 128
ROPE_THETA = 500000.0
ROPE_DIM = HEAD_DIM // 4
MOBA_BLOCK = 256
MOBA_TOPK = 3
REC_CHUNK = 64
REC_GROUP = 8
NORM_EPS = 1e-6
NEG_INF = -1e30

VMEM_LIMIT_BYTES = 60 * 1024 * 1024

F32 = jnp.float32
BF16 = jnp.bfloat16


def _params(n_grid_dims):
    return pltpu.CompilerParams(
        dimension_semantics=("arbitrary",) * n_grid_dims,
        vmem_limit_bytes=VMEM_LIMIT_BYTES,
    )


def _pick_block(dim, target, align):
    best = None
    for b in range(align, min(dim, target) + 1, align):
        if dim % b == 0:
            best = b
    assert best is not None, (dim, target, align)
    return best


def _sigmoid(x):
    return 0.5 * jnp.tanh(0.5 * x) + 0.5


def _silu(x):
    return x * _sigmoid(x)


def _rmsnorm_kernel(x_ref, w_ref, o_ref):
    x = x_ref[...].astype(F32)
    ms = jnp.mean(x * x, axis=-1, keepdims=True)
    o_ref[...] = (x * lax.rsqrt(ms + NORM_EPS) * w_ref[...]).astype(o_ref.dtype)


def _rmsnorm(x2d, w_row, *, block_rows=256):
    n, d = x2d.shape
    block_rows = _pick_block(n, block_rows, 8)
    return pl.pallas_call(
        _rmsnorm_kernel,
        grid=(n // block_rows,),
        in_specs=[
            pl.BlockSpec((block_rows, d), lambda i: (i, 0)),
            pl.BlockSpec((1, d), lambda i: (0, 0)),
        ],
        out_specs=pl.BlockSpec((block_rows, d), lambda i: (i, 0)),
        out_shape=jax.ShapeDtypeStruct((n, d), BF16),
        compiler_params=_params(1),
        name="rmsnorm_in",
    )(x2d, w_row)


def _matmul_kernel(a_ref, b_ref, o_ref):
    o_ref[...] = jnp.dot(a_ref[...], b_ref[...].astype(BF16),
                         preferred_element_type=F32).astype(o_ref.dtype)


def _matmul_cols(a, w, layer, out_dtype, *, col_lo, col_hi, skip_lo, skip_hi, bm=2048, bn=512,
                 name="matmul"):
    m, k = a.shape
    bm = _pick_block(m, bm, 8)
    bn = _pick_block(int(np.gcd.reduce([col_lo, col_hi, skip_lo, skip_hi, w.shape[2]])), bn, 128)
    t_lo, t_skip_lo, t_skip = col_lo // bn, skip_lo // bn, (skip_hi - skip_lo) // bn
    n = col_hi - col_lo - (skip_hi - skip_lo)

    def b_map(i, j):
        src = j + t_lo
        return (layer, 0, jnp.where(src >= t_skip_lo, src + t_skip, src))

    return pl.pallas_call(
        _matmul_kernel,
        grid=(m // bm, n // bn),
        in_specs=[
            pl.BlockSpec((bm, k), lambda i, j: (i, 0)),
            pl.BlockSpec((None, k, bn), b_map),
        ],
        out_specs=pl.BlockSpec((bm, bn), lambda i, j: (i, j)),
        out_shape=jax.ShapeDtypeStruct((m, n), out_dtype),
        compiler_params=_params(2),
        name=name,
    )(a, w)


def _attn_steps(q_ref, k_ref, v_ref, g_ref, cos_ref, sin_ref, o_ref, kaug_ref, vaug_ref):
    t = q_ref.shape[0]
    nb = t // MOBA_BLOCK
    nbp = max(16, -(-nb // 16) * 16)
    half = ROPE_DIM // 2
    cos = cos_ref[...]
    sin = sin_ref[...]
    lane = lax.broadcasted_iota(jnp.int32, (t, HEAD_DIM), 1)

    src = lax.broadcasted_iota(jnp.int32, (HEAD_DIM, HEAD_DIM), 0)
    dst = lax.broadcasted_iota(jnp.int32, (HEAD_DIM, HEAD_DIM), 1)
    pick = ((dst < half) & (src == dst + half)) | ((dst >= half) & (dst < 2 * half) & (src == dst - half))
    pick = pick.astype(BF16)

    def rope(x_bf16):
        partner = jnp.dot(x_bf16, pick, preferred_element_type=F32)
        return x_bf16.astype(F32) * cos + partner * sin

    k = rope(k_ref[...])
    q = rope(q_ref[...])
    q_s = (q * float(HEAD_DIM ** -0.5 * np.log2(np.e))).astype(BF16)

    blk_of_key = lax.broadcasted_iota(jnp.int32, (t, HEAD_DIM), 0) // MOBA_BLOCK
    kaug_ref[:, :HEAD_DIM] = k.astype(BF16)
    kaug_ref[:, HEAD_DIM:] = (blk_of_key == lane).astype(BF16)
    vaug_ref[:, :HEAD_DIM] = v_ref[...]
    vaug_ref[:, HEAD_DIM:] = jnp.ones((t, HEAD_DIM), BF16)

    k_mean = jnp.mean(k.reshape(nb, MOBA_BLOCK, HEAD_DIM), axis=1)
    k_mean = jnp.concatenate([k_mean, jnp.zeros((nbp - nb, HEAD_DIM), F32)], axis=0).astype(BF16)
    gate_t = lax.dot_general(k_mean, q_s, (((1,), (1,)), ((), ())), preferred_element_type=F32)

    row_blk = lax.broadcasted_iota(jnp.int32, (nbp, MOBA_BLOCK), 0)
    q_idx = lax.broadcasted_iota(jnp.int32, (MOBA_BLOCK, MOBA_BLOCK), 0)
    k_idx = lax.broadcasted_iota(jnp.int32, (MOBA_BLOCK, MOBA_BLOCK), 1)
    causal = k_idx <= q_idx

    def masked_scores(qb):
        rows = slice(qb * MOBA_BLOCK, (qb + 1) * MOBA_BLOCK)
        nk = (qb + 1) * MOBA_BLOCK
        q_blk = q_s[rows]
        if qb > MOBA_TOPK:
            g = gate_t[:, rows]
            rank = jnp.zeros((nbp, MOBA_BLOCK), F32)
            for m in range(qb):
                gm = g[m:m + 1, :]
                beats = (gm > g) | ((gm == g) & (m < row_blk))
                rank = rank + beats.astype(F32)
            keep = ((rank < MOBA_TOPK) & (row_blk < qb)) | (row_blk == qb)
            bias_t = jnp.where(keep, 0.0, NEG_INF)
            bias_t = jnp.concatenate([bias_t, jnp.zeros((HEAD_DIM - nbp, MOBA_BLOCK), F32)], axis=0)
            q_aug = jnp.concatenate([q_blk, bias_t.T.astype(BF16)], axis=1)
            s = lax.dot_general(q_aug, kaug_ref[:nk, :], (((1,), (1,)), ((), ())),
                                preferred_element_type=F32)
        else:
            s = lax.dot_general(q_blk, kaug_ref[:nk, :HEAD_DIM], (((1,), (1,)), ((), ())),
                                preferred_element_type=F32)
        s_own = jnp.where(causal, s[:, qb * MOBA_BLOCK:], NEG_INF)
        if qb > 0:
            return jnp.concatenate([s[:, :qb * MOBA_BLOCK], s_own], axis=1)
        return s_own

    def finish(qb, p):
        rows = slice(qb * MOBA_BLOCK, (qb + 1) * MOBA_BLOCK)
        oa = jnp.dot(p, vaug_ref[:(qb + 1) * MOBA_BLOCK, :], preferred_element_type=F32)
        o = oa[:, :HEAD_DIM] / oa[:, HEAD_DIM:]
        o_ref[rows, :] = (o * _silu(g_ref[rows, :].astype(F32))).astype(o_ref.dtype)

    s_cur = masked_scores(0)
    p_prev = None
    yield
    for qb in range(nb):
        s_next = masked_scores(qb + 1) if qb + 1 < nb else None
        if p_prev is not None:
            finish(qb - 1, p_prev)
        m_row = jnp.max(s_cur, axis=-1, keepdims=True)
        p_prev = jnp.exp2(s_cur - m_row).astype(BF16)
        s_cur = s_next
        yield
    finish(nb - 1, p_prev)
    yield


def _attn_kernel(*refs):
    for _ in _attn_steps(*refs):
        pass


def _moba_attention(proj, cos_t, sin_t, *, batch, seq, n_heads):
    n = batch * seq
    spec = lambda off: pl.BlockSpec((seq, HEAD_DIM), lambda b, h: (b, off + h))
    tbl = pl.BlockSpec((seq, HEAD_DIM), lambda b, h: (0, 0))
    return pl.pallas_call(
        _attn_kernel,
        grid=(batch, n_heads),
        in_specs=[spec(0), spec(n_heads), spec(2 * n_heads), spec(3 * n_heads), tbl, tbl],
        out_specs=pl.BlockSpec((seq, HEAD_DIM), lambda b, h: (b, h)),
        out_shape=jax.ShapeDtypeStruct((n, n_heads * HEAD_DIM), BF16),
        scratch_shapes=[
            pltpu.VMEM((seq, 2 * HEAD_DIM), BF16),
            pltpu.VMEM((seq, 2 * HEAD_DIM), BF16),
        ],
        compiler_params=_params(2),
        name="moba_attention",
    )(proj, proj, proj, proj, cos_t, sin_t)


def _rec_steps(layer, q_ref, z_ref, v_ref, g_ref, lbl_ref, nw_ref, o_ref,
               qin_ref, rhs_ref, upd_ref, cum_ref):
    t = q_ref.shape[0]
    n_chunks = t // REC_CHUNK

    logits = lbl_ref[...].astype(F32)
    e = jnp.exp(logits - jnp.max(logits, axis=0, keepdims=True))
    probs = e / jnp.sum(e, axis=0, keepdims=True)
    lb = jnp.sum(probs[:layer + 1], axis=0, keepdims=True)

    half_t = 0.5 * jnp.tanh(0.5 * z_ref[...])
    log2_f = jnp.log2(lb + (1.0 - lb) * (0.5 + half_t))
    k_r = (1.0 - lb) * (0.5 - half_t)
    q_r = _silu(q_ref[...].astype(F32)) * (HEAD_DIM ** -0.5)

    pos = lax.broadcasted_iota(jnp.int32, (t, HEAD_DIM), 0) % REC_CHUNK
    cum = log2_f
    shift = 1
    while shift < REC_CHUNK:
        cum = cum + jnp.where(pos >= shift, pltpu.roll(cum, shift, 0), 0.0)
        shift *= 2
    cum_ref[...] = cum

    mid = REC_CHUNK // 2 - 1
    cum3 = cum.reshape(n_chunks, REC_CHUNK, HEAD_DIM)
    rel = cum3 - cum3[:, mid:mid + 1, :]
    qin_ref[...] = (q_r.reshape(cum3.shape) * jnp.exp2(rel)).reshape(t, HEAD_DIM).astype(BF16)
    rhs_ref[:, HEAD_DIM:, :] = (k_r.reshape(cum3.shape) * jnp.exp2(-rel)).astype(BF16)
    yield

    def chunk_rows(c):
        return slice(c * REC_CHUNK, (c + 1) * REC_CHUNK)

    for c in range(n_chunks):
        upd_ref[c] = lax.dot_general(v_ref[chunk_rows(c), :], rhs_ref[c, HEAD_DIM:, :],
                                     (((0,), (0,)), ((), ())), preferred_element_type=F32)
        if (c + 1) % REC_GROUP == 0:
            yield

    state_t = jnp.zeros((HEAD_DIM, HEAD_DIM), F32)
    for c in range(n_chunks):
        c_mid = cum_ref[c * REC_CHUNK + mid:c * REC_CHUNK + mid + 1, :]
        c_last = cum_ref[(c + 1) * REC_CHUNK - 1:(c + 1) * REC_CHUNK, :]
        rhs_ref[c, :HEAD_DIM, :] = (state_t * jnp.exp2(c_mid)).astype(BF16)
        state_t = state_t * jnp.exp2(c_last) + upd_ref[c] * jnp.exp2(c_last - c_mid)
        if (c + 1) % REC_GROUP == 0:
            yield

    ri = lax.broadcasted_iota(jnp.int32, (REC_CHUNK, REC_CHUNK), 0)
    ci = lax.broadcasted_iota(jnp.int32, (REC_CHUNK, REC_CHUNK), 1)
    tril = ri >= ci
    nw = nw_ref[...].astype(F32)
    for g0 in range(0, n_chunks, REC_GROUP):
        group = range(g0, min(g0 + REC_GROUP, n_chunks))
        both = [lax.dot_general(qin_ref[chunk_rows(c), :], rhs_ref[c], (((1,), (1,)), ((), ())),
                                preferred_element_type=F32) for c in group]
        for c, b in zip(group, both):
            rows = chunk_rows(c)
            scores = jnp.where(tril, b[:, HEAD_DIM:], 0.0).astype(BF16)
            o = b[:, :HEAD_DIM] + jnp.dot(scores, v_ref[rows, :], preferred_element_type=F32)
            o = o * lax.rsqrt(jnp.mean(o * o, axis=-1, keepdims=True) + NORM_EPS)
            o = o * nw * _silu(g_ref[rows, :].astype(F32))
            o_ref[rows, :] = o.astype(o_ref.dtype)
        yield


def _rec_kernel(layer, *refs):
    for _ in _rec_steps(layer, *refs):
        pass


def _mixer_kernel(layer, aq, ak, av, ag, cos, sin, rq, rz, rv, rg, lbl, nw, attn_o, rec_o,
                  kaug, vaug, qin, rhs, upd, cum):
    streams = [_attn_steps(aq, ak, av, ag, cos, sin, attn_o, kaug, vaug),
               _rec_steps(layer, rq, rz, rv, rg, lbl, nw, rec_o, qin, rhs, upd, cum)]
    stop = object()
    while streams:
        streams = [g for g in streams if next(g, stop) is not stop]


def _mixers(proj, z, lbl, nw_row, cos_t, sin_t, *, layer, batch, seq, n_heads):
    n = batch * seq
    spec = lambda off: pl.BlockSpec((seq, HEAD_DIM), lambda b, h: (b, off + h))
    tbl = pl.BlockSpec((seq, HEAD_DIM), lambda b, h: (0, 0))
    n_chunks = seq // REC_CHUNK
    out = jax.ShapeDtypeStruct((n, n_heads * HEAD_DIM), BF16)
    return pl.pallas_call(
        functools.partial(_mixer_kernel, layer),
        grid=(batch, n_heads),
        in_specs=[
            spec(0), spec(n_heads), spec(2 * n_heads), spec(3 * n_heads), tbl, tbl,
            spec(4 * n_heads),
            pl.BlockSpec((seq, HEAD_DIM), lambda b, h: (b, h)),
            spec(5 * n_heads), spec(6 * n_heads),
            pl.BlockSpec((lbl.shape[0], HEAD_DIM), lambda b, h: (0, h)),
            pl.BlockSpec((1, HEAD_DIM), lambda b, h: (0, h)),
        ],
        out_specs=[pl.BlockSpec((seq, HEAD_DIM), lambda b, h: (b, h))] * 2,
        out_shape=[out, out],
        scratch_shapes=[
            pltpu.VMEM((seq, 2 * HEAD_DIM), BF16),
            pltpu.VMEM((seq, 2 * HEAD_DIM), BF16),
            pltpu.VMEM((seq, HEAD_DIM), BF16),
            pltpu.VMEM((n_chunks, HEAD_DIM + REC_CHUNK, HEAD_DIM), BF16),
            pltpu.VMEM((n_chunks, HEAD_DIM, HEAD_DIM), F32),
            pltpu.VMEM((seq, HEAD_DIM), F32),
        ],
        compiler_params=_params(2),
        name="mixers",
    )(proj, proj, proj, proj, cos_t, sin_t, proj, z, proj, proj, lbl, nw_row)


def _hgrn2(proj, z, lbl, nw_row, *, layer, batch, seq, n_attn_heads, n_heads):
    n = batch * seq
    base = 4 * n_attn_heads
    spec = lambda off: pl.BlockSpec((seq, HEAD_DIM), lambda b, h: (b, off + h))
    return pl.pallas_call(
        functools.partial(_rec_kernel, layer),
        grid=(batch, n_heads),
        in_specs=[
            spec(base),
            pl.BlockSpec((seq, HEAD_DIM), lambda b, h: (b, h)),
            spec(base + n_heads),
            spec(base + 2 * n_heads),
            pl.BlockSpec((lbl.shape[0], HEAD_DIM), lambda b, h: (0, h)),
            pl.BlockSpec((1, HEAD_DIM), lambda b, h: (0, h)),
        ],
        out_specs=pl.BlockSpec((seq, HEAD_DIM), lambda b, h: (b, h)),
        out_shape=jax.ShapeDtypeStruct((n, n_heads * HEAD_DIM), BF16),
        scratch_shapes=[
            pltpu.VMEM((seq, HEAD_DIM), BF16),
            pltpu.VMEM((seq // REC_CHUNK, HEAD_DIM + REC_CHUNK, HEAD_DIM), BF16),
            pltpu.VMEM((seq // REC_CHUNK, HEAD_DIM, HEAD_DIM), F32),
            pltpu.VMEM((seq, HEAD_DIM), F32),
        ],
        compiler_params=_params(2),
        name="hgrn2_recurrence",
    )(proj, z, proj, proj, lbl, nw_row)


def _out_kernel(final_norm, n_attn_k, n_k, attn_ref, rec_ref, w_ref, x_ref, nw_ref, o_ref):
    kk = pl.program_id(1)
    cw = x_ref.shape[1]

    def partial_product():
        lhs = jnp.where(kk < n_attn_k, attn_ref[...], rec_ref[...])
        return jnp.dot(lhs, w_ref[...], preferred_element_type=F32)

    def add_residual(j):
        o_ref[:, j * cw:(j + 1) * cw] += x_ref[...]

    def finish():
        if final_norm:
            h = o_ref[...]
            ms = jnp.mean(h * h, axis=-1, keepdims=True)
            o_ref[...] = h * lax.rsqrt(ms + NORM_EPS) * nw_ref[...]

    if n_k == 1:
        o_ref[...] = partial_product()
        add_residual(0)
        finish()
        return

    @pl.when(kk == 0)
    def _():
        o_ref[...] = partial_product()
        add_residual(0)

    @pl.when((kk > 0) & (kk < n_k - 1))
    def _():
        o_ref[...] += partial_product()

    for j in range(1, n_k - 1):
        @pl.when(kk == j)
        def _(j=j):
            add_residual(j)

    @pl.when(kk == n_k - 1)
    def _():
        o_ref[...] += partial_product()
        add_residual(n_k - 1)
        finish()


def _out_proj(attn, rec, w_bf16, x2d, nw_row, *, final_norm, bm=1024, bk=512):
    n, d = x2d.shape
    a_w = attn.shape[1]
    r_w = rec.shape[1]
    bm = _pick_block(n, bm, 8)
    bk = _pick_block(int(np.gcd(a_w, r_w)), bk, 128)
    n_attn_k = a_w // bk
    n_k = (a_w + r_w) // bk
    assert d % (n_k * 128) == 0
    return pl.pallas_call(
        functools.partial(_out_kernel, final_norm, n_attn_k, n_k),
        grid=(n // bm, n_k),
        in_specs=[
            pl.BlockSpec((bm, bk), lambda i, k: (i, jnp.minimum(k, n_attn_k - 1))),
            pl.BlockSpec((bm, bk), lambda i, k: (i, jnp.maximum(k - n_attn_k, 0))),
            pl.BlockSpec((bk, d), lambda i, k: (k, 0)),
            pl.BlockSpec((bm, d // n_k), lambda i, k: (i, k)),
            pl.BlockSpec((1, d), lambda i, k: (0, 0)),
        ],
        out_specs=pl.BlockSpec((bm, d), lambda i, k: (i, 0)),
        out_shape=jax.ShapeDtypeStruct((n, d), F32),
        compiler_params=_params(2),
        name="out_proj",
    )(attn, rec, w_bf16, x2d, nw_row)


def _rope_tables(seq):
    half = ROPE_DIM // 2
    inv_freq = jnp.power(ROPE_THETA, -jnp.arange(half, dtype=F32) / half)
    ang = jnp.arange(seq, dtype=F32)[:, None] * inv_freq[None, :]
    cos, sin = jnp.cos(ang), jnp.sin(ang)
    rest = HEAD_DIM - ROPE_DIM
    cos_t = jnp.concatenate([cos, cos, jnp.ones((seq, rest), F32)], axis=1)
    sin_t = jnp.concatenate([-sin, sin, jnp.zeros((seq, rest), F32)], axis=1)
    return cos_t, sin_t


def kernel(x, norm_w, w_in, rec_lower_bound_logits, rec_out_norm_w, w_out, final_norm_w):
    batch, seq, d_model = x.shape
    depth = norm_w.shape[0]
    mix = w_out.shape[1]
    attn_w = mix // 2
    rec_w = mix - attn_w
    n_ah = attn_w // HEAD_DIM
    n_rh = rec_w // HEAD_DIM
    key_w = n_rh * HEAD_DIM
    f_lo = 4 * attn_w + key_w
    f_hi = f_lo + key_w
    assert seq % MOBA_BLOCK == 0 and seq % REC_CHUNK == 0
    assert w_in.shape[2] == f_hi + 2 * rec_w

    cos_t, sin_t = _rope_tables(seq)
    h = x.reshape(batch * seq, d_model)
    for layer in range(depth):
        in_w = w_in.shape[2]
        u = _rmsnorm(h, norm_w[layer].reshape(1, d_model))
        proj = _matmul_cols(u, w_in, layer, BF16, col_lo=0, col_hi=in_w, skip_lo=f_lo, skip_hi=f_hi,
                            name="in_proj_main")
        z = _matmul_cols(u, w_in, layer, F32, col_lo=f_lo, col_hi=f_hi, skip_lo=in_w, skip_hi=in_w,
                         name="in_proj_forget")
        rec_nw = rec_out_norm_w[layer].reshape(1, rec_w)
        if n_ah == n_rh:
            attn, rec = _mixers(proj, z, rec_lower_bound_logits, rec_nw, cos_t, sin_t,
                                layer=layer, batch=batch, seq=seq, n_heads=n_ah)
        else:
            attn = _moba_attention(proj, cos_t, sin_t, batch=batch, seq=seq, n_heads=n_ah)
            rec = _hgrn2(proj, z, rec_lower_bound_logits, rec_nw,
                         layer=layer, batch=batch, seq=seq, n_attn_heads=n_ah, n_heads=n_rh)
        last = layer == depth - 1
        h = _out_proj(attn, rec, w_out[layer].astype(BF16), h, final_norm_w.reshape(1, d_model),
                      final_norm=last)
    return h.reshape(batch, seq, d_model)
```

```python
import functools

import jax
import jax.numpy as jnp
import numpy as np
from jax import lax
from jax.experimental import pallas as pl
from jax.experimental.pallas import tpu as pltpu

HEAD_DIM = 128
ROPE_THETA = 500000.0
ROPE_DIM = HEAD_DIM // 4
MOBA_BLOCK = 256
MOBA_TOPK = 3
REC_CHUNK = 64
REC_GROUP = 8
NORM_EPS = 1e-6
NEG_INF = -1e30

VMEM_LIMIT_BYTES = 60 * 1024 * 1024

F32 = jnp.float32
BF16 = jnp.bfloat16


def _params(n_grid_dims):
    return pltpu.CompilerParams(
        dimension_semantics=("arbitrary",) * n_grid_dims,
        vmem_limit_bytes=VMEM_LIMIT_BYTES,
    )


def _pick_block(dim, target, align):
    best = None
    for b in range(align, min(dim, target) + 1, align):
        if dim % b == 0:
            best = b
    assert best is not None, (dim, target, align)
    return best


def _sigmoid(x):
    return 0.5 * jnp.tanh(0.5 * x) + 0.5


def _silu(x):
    return x * _sigmoid(x)


def _norm_proj_kernel(x_ref, nw_ref, w_ref, o_ref, u_even, u_odd):
    i = pl.program_id(0)
    j = pl.program_id(1)
    strip = x_ref.shape[0]

    def normalize_into(u_dst):
        x = x_ref[...]
        ms = jnp.mean(x * x, axis=-1, keepdims=True)
        rows = pl.ds(pl.multiple_of(j * strip, strip), strip)
        u_dst[rows, :] = (x * lax.rsqrt(ms + NORM_EPS) * nw_ref[...]).astype(BF16)

    def project(u_src):
        o_ref[...] = jnp.dot(u_src[...], w_ref[...].astype(BF16),
                             preferred_element_type=F32).astype(o_ref.dtype)

    @pl.when(i == 0)
    def _():
        normalize_into(u_even)

    @pl.when((i > 0) & (i % 2 == 1))
    def _():
        project(u_even)
        normalize_into(u_odd)

    @pl.when((i > 0) & (i % 2 == 0))
    def _():
        project(u_odd)
        normalize_into(u_even)


def _norm_proj(x2d, nw_row, w, layer, *, bm=2048, bn=512):
    n, d = x2d.shape
    in_w = w.shape[2]
    bm = _pick_block(n, bm, 8)
    bn = _pick_block(in_w, bn, 128)
    n_i, n_j = n // bm, in_w // bn
    strip = bm // n_j
    assert strip * n_j == bm and strip % 8 == 0, (bm, n_j)
    last_strip = n // strip - 1
    return pl.pallas_call(
        _norm_proj_kernel,
        grid=(n_i + 1, n_j),
        in_specs=[
            pl.BlockSpec((strip, d), lambda i, j: (jnp.minimum(i * n_j + j, last_strip), 0)),
            pl.BlockSpec((1, d), lambda i, j: (0, 0)),
            pl.BlockSpec((None, d, bn), lambda i, j: (layer, 0, jnp.where(i == 0, 0, j))),
        ],
        out_specs=pl.BlockSpec((bm, bn), lambda i, j: (jnp.maximum(i - 1, 0), jnp.where(i == 0, 0, j))),
        out_shape=jax.ShapeDtypeStruct((n, in_w), BF16),
        scratch_shapes=[pltpu.VMEM((bm, d), BF16), pltpu.VMEM((bm, d), BF16)],
        compiler_params=_params(2),
        name="norm_in_proj",
    )(x2d, nw_row, w)


def _attn_steps(q_ref, k_ref, v_ref, g_ref, cos_ref, sin_ref, o_ref, kaug_ref, vaug_ref):
    t = q_ref.shape[0]
    nb = t // MOBA_BLOCK
    nbp = max(16, -(-nb // 16) * 16)
    half = ROPE_DIM // 2
    cos = cos_ref[...]
    sin = sin_ref[...]
    lane = lax.broadcasted_iota(jnp.int32, (t, HEAD_DIM), 1)

    src = lax.broadcasted_iota(jnp.int32, (HEAD_DIM, HEAD_DIM), 0)
    dst = lax.broadcasted_iota(jnp.int32, (HEAD_DIM, HEAD_DIM), 1)
    pick = ((dst < half) & (src == dst + half)) | ((dst >= half) & (dst < 2 * half) & (src == dst - half))
    pick = pick.astype(BF16)

    def rope(x_bf16):
        partner = jnp.dot(x_bf16, pick, preferred_element_type=F32)
        return x_bf16.astype(F32) * cos + partner * sin

    k = rope(k_ref[...])
    q = rope(q_ref[...])
    q_s = (q * float(HEAD_DIM ** -0.5 * np.log2(np.e))).astype(BF16)

    blk_of_key = lax.broadcasted_iota(jnp.int32, (t, HEAD_DIM), 0) // MOBA_BLOCK
    kaug_ref[:, :HEAD_DIM] = k.astype(BF16)
    kaug_ref[:, HEAD_DIM:] = (blk_of_key == lane).astype(BF16)
    vaug_ref[:, :HEAD_DIM] = v_ref[...]
    vaug_ref[:, HEAD_DIM:] = jnp.ones((t, HEAD_DIM), BF16)

    k_mean = jnp.mean(k.reshape(nb, MOBA_BLOCK, HEAD_DIM), axis=1)
    k_mean = jnp.concatenate([k_mean, jnp.zeros((nbp - nb, HEAD_DIM), F32)], axis=0).astype(BF16)
    gate_t = lax.dot_general(k_mean, q_s, (((1,), (1,)), ((), ())), preferred_element_type=F32)

    row_blk = lax.broadcasted_iota(jnp.int32, (nbp, MOBA_BLOCK), 0)
    q_idx = lax.broadcasted_iota(jnp.int32, (MOBA_BLOCK, MOBA_BLOCK), 0)
    k_idx = lax.broadcasted_iota(jnp.int32, (MOBA_BLOCK, MOBA_BLOCK), 1)
    causal = k_idx <= q_idx

    def masked_scores(qb):
        rows = slice(qb * MOBA_BLOCK, (qb + 1) * MOBA_BLOCK)
        nk = (qb + 1) * MOBA_BLOCK
        q_blk = q_s[rows]
        if qb > MOBA_TOPK:
            g = gate_t[:, rows]
            rank = jnp.zeros((nbp, MOBA_BLOCK), F32)
            for m in range(qb):
                gm = g[m:m + 1, :]
                beats = (gm > g) | ((gm == g) & (m < row_blk))
                rank = rank + beats.astype(F32)
            keep = ((rank < MOBA_TOPK) & (row_blk < qb)) | (row_blk == qb)
            bias_t = jnp.where(keep, 0.0, NEG_INF)
            bias_t = jnp.concatenate([bias_t, jnp.zeros((HEAD_DIM - nbp, MOBA_BLOCK), F32)], axis=0)
            q_aug = jnp.concatenate([q_blk, bias_t.T.astype(BF16)], axis=1)
            s = lax.dot_general(q_aug, kaug_ref[:nk, :], (((1,), (1,)), ((), ())),
                                preferred_element_type=F32)
        else:
            s = lax.dot_general(q_blk, kaug_ref[:nk, :HEAD_DIM], (((1,), (1,)), ((), ())),
                                preferred_element_type=F32)
        s_own = jnp.where(causal, s[:, qb * MOBA_BLOCK:], NEG_INF)
        if qb > 0:
            return jnp.concatenate([s[:, :qb * MOBA_BLOCK], s_own], axis=1)
        return s_own

    def finish(qb, p):
        rows = slice(qb * MOBA_BLOCK, (qb + 1) * MOBA_BLOCK)
        oa = jnp.dot(p, vaug_ref[:(qb + 1) * MOBA_BLOCK, :], preferred_element_type=F32)
        o = oa[:, :HEAD_DIM] / oa[:, HEAD_DIM:]
        o_ref[rows, :] = (o * _silu(g_ref[rows, :].astype(F32))).astype(o_ref.dtype)

    s_cur = masked_scores(0)
    p_prev = None
    yield
    for qb in range(nb):
        s_next = masked_scores(qb + 1) if qb + 1 < nb else None
        if p_prev is not None:
            finish(qb - 1, p_prev)
        m_row = jnp.max(s_cur, axis=-1, keepdims=True)
        p_prev = jnp.exp2(s_cur - m_row).astype(BF16)
        s_cur = s_next
        yield
    finish(nb - 1, p_prev)
    yield


def _attn_kernel(*refs):
    for _ in _attn_steps(*refs):
        pass


def _moba_attention(proj, cos_t, sin_t, *, batch, seq, n_heads):
    n = batch * seq
    spec = lambda off: pl.BlockSpec((seq, HEAD_DIM), lambda b, h: (b, off + h))
    tbl = pl.BlockSpec((seq, HEAD_DIM), lambda b, h: (0, 0))
    return pl.pallas_call(
        _attn_kernel,
        grid=(batch, n_heads),
        in_specs=[spec(0), spec(n_heads), spec(2 * n_heads), spec(3 * n_heads), tbl, tbl],
        out_specs=pl.BlockSpec((seq, HEAD_DIM), lambda b, h: (b, h)),
        out_shape=jax.ShapeDtypeStruct((n, n_heads * HEAD_DIM), BF16),
        scratch_shapes=[
            pltpu.VMEM((seq, 2 * HEAD_DIM), BF16),
            pltpu.VMEM((seq, 2 * HEAD_DIM), BF16),
        ],
        compiler_params=_params(2),
        name="moba_attention",
    )(proj, proj, proj, proj, cos_t, sin_t)


def _rec_steps(layer, q_ref, z_ref, v_ref, g_ref, lbl_ref, nw_ref, o_ref,
               qin_ref, rhs_ref, upd_ref, cum_ref):
    t = q_ref.shape[0]
    n_chunks = t // REC_CHUNK

    logits = lbl_ref[...].astype(F32)
    e = jnp.exp(logits - jnp.max(logits, axis=0, keepdims=True))
    probs = e / jnp.sum(e, axis=0, keepdims=True)
    lb = jnp.sum(probs[:layer + 1], axis=0, keepdims=True)

    half_t = 0.5 * jnp.tanh(0.5 * z_ref[...].astype(F32))
    log2_f = jnp.log2(lb + (1.0 - lb) * (0.5 + half_t))
    k_r = (1.0 - lb) * (0.5 - half_t)
    q_r = _silu(q_ref[...].astype(F32)) * (HEAD_DIM ** -0.5)

    pos = lax.broadcasted_iota(jnp.int32, (t, HEAD_DIM), 0) % REC_CHUNK
    cum = log2_f
    shift = 1
    while shift < REC_CHUNK:
        cum = cum + jnp.where(pos >= shift, pltpu.roll(cum, shift, 0), 0.0)
        shift *= 2
    cum_ref[...] = cum

    mid = REC_CHUNK // 2 - 1
    cum3 = cum.reshape(n_chunks, REC_CHUNK, HEAD_DIM)
    rel = cum3 - cum3[:, mid:mid + 1, :]
    qin_ref[...] = (q_r.reshape(cum3.shape) * jnp.exp2(rel)).reshape(t, HEAD_DIM).astype(BF16)
    rhs_ref[:, HEAD_DIM:, :] = (k_r.reshape(cum3.shape) * jnp.exp2(-rel)).astype(BF16)
    yield

    def chunk_rows(c):
        return slice(c * REC_CHUNK, (c + 1) * REC_CHUNK)

    for c in range(n_chunks):
        upd_ref[c] = lax.dot_general(v_ref[chunk_rows(c), :], rhs_ref[c, HEAD_DIM:, :],
                                     (((0,), (0,)), ((), ())), preferred_element_type=F32)
        if (c + 1) % REC_GROUP == 0:
            yield

    state_t = jnp.zeros((HEAD_DIM, HEAD_DIM), F32)
    for c in range(n_chunks):
        c_mid = cum_ref[c * REC_CHUNK + mid:c * REC_CHUNK + mid + 1, :]
        c_last = cum_ref[(c + 1) * REC_CHUNK - 1:(c + 1) * REC_CHUNK, :]
        rhs_ref[c, :HEAD_DIM, :] = (state_t * jnp.exp2(c_mid)).astype(BF16)
        state_t = state_t * jnp.exp2(c_last) + upd_ref[c] * jnp.exp2(c_last - c_mid)
        if (c + 1) % REC_GROUP == 0:
            yield

    ri = lax.broadcasted_iota(jnp.int32, (REC_CHUNK, REC_CHUNK), 0)
    ci = lax.broadcasted_iota(jnp.int32, (REC_CHUNK, REC_CHUNK), 1)
    tril = ri >= ci
    nw = nw_ref[...].astype(F32)
    for g0 in range(0, n_chunks, REC_GROUP):
        group = range(g0, min(g0 + REC_GROUP, n_chunks))
        both = [lax.dot_general(qin_ref[chunk_rows(c), :], rhs_ref[c], (((1,), (1,)), ((), ())),
                                preferred_element_type=F32) for c in group]
        for c, b in zip(group, both):
            rows = chunk_rows(c)
            scores = jnp.where(tril, b[:, HEAD_DIM:], 0.0).astype(BF16)
            o = b[:, :HEAD_DIM] + jnp.dot(scores, v_ref[rows, :], preferred_element_type=F32)
            o = o * lax.rsqrt(jnp.mean(o * o, axis=-1, keepdims=True) + NORM_EPS)
            o = o * nw * _silu(g_ref[rows, :].astype(F32))
            o_ref[rows, :] = o.astype(o_ref.dtype)
        yield


def _rec_kernel(layer, *refs):
    for _ in _rec_steps(layer, *refs):
        pass


def _mixer_kernel(layer, aq, ak, av, ag, cos, sin, rq, rz, rv, rg, lbl, nw, attn_o, rec_o,
                  kaug, vaug, qin, rhs, upd, cum):
    streams = [_attn_steps(aq, ak, av, ag, cos, sin, attn_o, kaug, vaug),
               _rec_steps(layer, rq, rz, rv, rg, lbl, nw, rec_o, qin, rhs, upd, cum)]
    stop = object()
    while streams:
        streams = [g for g in streams if next(g, stop) is not stop]


def _mixers(proj, lbl, nw_row, cos_t, sin_t, *, layer, batch, seq, n_heads):
    n = batch * seq
    spec = lambda off: pl.BlockSpec((seq, HEAD_DIM), lambda b, h: (b, off + h))
    tbl = pl.BlockSpec((seq, HEAD_DIM), lambda b, h: (0, 0))
    n_chunks = seq // REC_CHUNK
    out = jax.ShapeDtypeStruct((n, n_heads * HEAD_DIM), BF16)
    return pl.pallas_call(
        functools.partial(_mixer_kernel, layer),
        grid=(batch, n_heads),
        in_specs=[
            spec(0), spec(n_heads), spec(2 * n_heads), spec(3 * n_heads), tbl, tbl,
            spec(4 * n_heads), spec(5 * n_heads), spec(6 * n_heads), spec(7 * n_heads),
            pl.BlockSpec((lbl.shape[0], HEAD_DIM), lambda b, h: (0, h)),
            pl.BlockSpec((1, HEAD_DIM), lambda b, h: (0, h)),
        ],
        out_specs=[pl.BlockSpec((seq, HEAD_DIM), lambda b, h: (b, h))] * 2,
        out_shape=[out, out],
        scratch_shapes=[
            pltpu.VMEM((seq, 2 * HEAD_DIM), BF16),
            pltpu.VMEM((seq, 2 * HEAD_DIM), BF16),
            pltpu.VMEM((seq, HEAD_DIM), BF16),
            pltpu.VMEM((n_chunks, HEAD_DIM + REC_CHUNK, HEAD_DIM), BF16),
            pltpu.VMEM((n_chunks, HEAD_DIM, HEAD_DIM), F32),
            pltpu.VMEM((seq, HEAD_DIM), F32),
        ],
        compiler_params=_params(2),
        name="mixers",
    )(proj, proj, proj, proj, cos_t, sin_t, proj, proj, proj, proj, lbl, nw_row)


def _hgrn2(proj, lbl, nw_row, *, layer, batch, seq, n_attn_heads, n_heads):
    n = batch * seq
    base = 4 * n_attn_heads
    spec = lambda off: pl.BlockSpec((seq, HEAD_DIM), lambda b, h: (b, off + h))
    return pl.pallas_call(
        functools.partial(_rec_kernel, layer),
        grid=(batch, n_heads),
        in_specs=[
            spec(base),
            spec(base + n_heads),
            spec(base + 2 * n_heads),
            spec(base + 3 * n_heads),
            pl.BlockSpec((lbl.shape[0], HEAD_DIM), lambda b, h: (0, h)),
            pl.BlockSpec((1, HEAD_DIM), lambda b, h: (0, h)),
        ],
        out_specs=pl.BlockSpec((seq, HEAD_DIM), lambda b, h: (b, h)),
        out_shape=jax.ShapeDtypeStruct((n, n_heads * HEAD_DIM), BF16),
        scratch_shapes=[
            pltpu.VMEM((seq, HEAD_DIM), BF16),
            pltpu.VMEM((seq // REC_CHUNK, HEAD_DIM + REC_CHUNK, HEAD_DIM), BF16),
            pltpu.VMEM((seq // REC_CHUNK, HEAD_DIM, HEAD_DIM), F32),
            pltpu.VMEM((seq, HEAD_DIM), F32),
        ],
        compiler_params=_params(2),
        name="hgrn2_recurrence",
    )(proj, proj, proj, proj, lbl, nw_row)


def _out_kernel(final_norm, n_attn_k, n_k, attn_ref, rec_ref, w_ref, x_ref, nw_ref, o_ref):
    kk = pl.program_id(1)
    cw = x_ref.shape[1]

    def partial_product():
        lhs = jnp.where(kk < n_attn_k, attn_ref[...], rec_ref[...])
        return jnp.dot(lhs, w_ref[...], preferred_element_type=F32)

    def add_residual(j):
        o_ref[:, j * cw:(j + 1) * cw] += x_ref[...]

    def finish():
        if final_norm:
            h = o_ref[...]
            ms = jnp.mean(h * h, axis=-1, keepdims=True)
            o_ref[...] = h * lax.rsqrt(ms + NORM_EPS) * nw_ref[...]

    if n_k == 1:
        o_ref[...] = partial_product()
        add_residual(0)
        finish()
        return

    @pl.when(kk == 0)
    def _():
        o_ref[...] = partial_product()
        add_residual(0)

    @pl.when((kk > 0) & (kk < n_k - 1))
    def _():
        o_ref[...] += partial_product()

    for j in range(1, n_k - 1):
        @pl.when(kk == j)
        def _(j=j):
            add_residual(j)

    @pl.when(kk == n_k - 1)
    def _():
        o_ref[...] += partial_product()
        add_residual(n_k - 1)
        finish()


def _out_proj(attn, rec, w_bf16, x2d, nw_row, *, final_norm, bm=1024, bk=512):
    n, d = x2d.shape
    a_w = attn.shape[1]
    r_w = rec.shape[1]
    bm = _pick_block(n, bm, 8)
    bk = _pick_block(int(np.gcd(a_w, r_w)), bk, 128)
    n_attn_k = a_w // bk
    n_k = (a_w + r_w) // bk
    assert d % (n_k * 128) == 0
    return pl.pallas_call(
        functools.partial(_out_kernel, final_norm, n_attn_k, n_k),
        grid=(n // bm, n_k),
        in_specs=[
            pl.BlockSpec((bm, bk), lambda i, k: (i, jnp.minimum(k, n_attn_k - 1))),
            pl.BlockSpec((bm, bk), lambda i, k: (i, jnp.maximum(k - n_attn_k, 0))),
            pl.BlockSpec((bk, d), lambda i, k: (k, 0)),
            pl.BlockSpec((bm, d // n_k), lambda i, k: (i, k)),
            pl.BlockSpec((1, d), lambda i, k: (0, 0)),
        ],
        out_specs=pl.BlockSpec((bm, d), lambda i, k: (i, 0)),
        out_shape=jax.ShapeDtypeStruct((n, d), F32),
        compiler_params=_params(2),
        name="out_proj",
    )(attn, rec, w_bf16, x2d, nw_row)


def _rope_tables(seq):
    half = ROPE_DIM // 2
    inv_freq = jnp.power(ROPE_THETA, -jnp.arange(half, dtype=F32) / half)
    ang = jnp.arange(seq, dtype=F32)[:, None] * inv_freq[None, :]
    cos, sin = jnp.cos(ang), jnp.sin(ang)
    rest = HEAD_DIM - ROPE_DIM
    cos_t = jnp.concatenate([cos, cos, jnp.ones((seq, rest), F32)], axis=1)
    sin_t = jnp.concatenate([-sin, sin, jnp.zeros((seq, rest), F32)], axis=1)
    return cos_t, sin_t


def kernel(x, norm_w, w_in, rec_lower_bound_logits, rec_out_norm_w, w_out, final_norm_w):
    batch, seq, d_model = x.shape
    depth = norm_w.shape[0]
    mix = w_out.shape[1]
    attn_w = mix // 2
    rec_w = mix - attn_w
    n_ah = attn_w // HEAD_DIM
    n_rh = rec_w // HEAD_DIM
    key_w = n_rh * HEAD_DIM
    assert seq % MOBA_BLOCK == 0 and seq % REC_CHUNK == 0
    assert w_in.shape[2] == 4 * attn_w + 2 * key_w + 2 * rec_w

    cos_t, sin_t = _rope_tables(seq)
    h = x.reshape(batch * seq, d_model)
    for layer in range(depth):
        proj = _norm_proj(h, norm_w[layer].reshape(1, d_model), w_in, layer)
        rec_nw = rec_out_norm_w[layer].reshape(1, rec_w)
        if n_ah == n_rh:
            attn, rec = _mixers(proj, rec_lower_bound_logits, rec_nw, cos_t, sin_t,
                                layer=layer, batch=batch, seq=seq, n_heads=n_ah)
        else:
            attn = _moba_attention(proj, cos_t, sin_t, batch=batch, seq=seq, n_heads=n_ah)
            rec = _hgrn2(proj, rec_lower_bound_logits, rec_nw,
                         layer=layer, batch=batch, seq=seq, n_attn_heads=n_ah, n_heads=n_rh)
        last = layer == depth - 1
        h = _out_proj(attn, rec, w_out[layer].astype(BF16), h, final_norm_w.reshape(1, d_model),
                      final_norm=last)
    return h.reshape(batch, seq, d_model)
```

```python
import functools

import jax
import jax.numpy as jnp
import numpy as np
from jax import lax
from jax.experimental import pallas as pl
from jax.experimental.pallas import tpu as pltpu

HEAD_DIM = 128
ROPE_THETA = 500000.0
ROPE_DIM = HEAD_DIM // 4
MOBA_BLOCK = 256
MOBA_TOPK = 3
REC_CHUNK = 64
REC_GROUP = 8
NORM_EPS = 1e-6
NEG_INF = -1e30

VMEM_LIMIT_BYTES = 60 * 1024 * 1024

F32 = jnp.float32
BF16 = jnp.bfloat16


def _params(n_grid_dims):
    return pltpu.CompilerParams(
        dimension_semantics=("arbitrary",) * n_grid_dims,
        vmem_limit_bytes=VMEM_LIMIT_BYTES,
    )


def _pick_block(dim, target, align):
    best = None
    for b in range(align, min(dim, target) + 1, align):
        if dim % b == 0:
            best = b
    assert best is not None, (dim, target, align)
    return best


def _sigmoid(x):
    return 0.5 * jnp.tanh(0.5 * x) + 0.5


def _silu(x):
    return x * _sigmoid(x)


def _norm_proj_kernel(x_ref, nw_ref, w_ref, o_ref, u_even, u_odd):
    i = pl.program_id(0)
    j = pl.program_id(1)
    strip = x_ref.shape[0]

    def normalize_into(u_dst):
        x = x_ref[...]
        ms = jnp.mean(x * x, axis=-1, keepdims=True)
        rows = pl.ds(pl.multiple_of(j * strip, strip), strip)
        u_dst[rows, :] = (x * lax.rsqrt(ms + NORM_EPS) * nw_ref[...]).astype(BF16)

    def project(u_src):
        acc = jnp.dot(u_src[...], w_ref[...].astype(BF16), preferred_element_type=F32)
        for c in range(o_ref.shape[0]):
            o_ref[c] = acc[:, c * HEAD_DIM:(c + 1) * HEAD_DIM].astype(o_ref.dtype)

    @pl.when(i == 0)
    def _():
        normalize_into(u_even)

    @pl.when((i > 0) & (i % 2 == 1))
    def _():
        project(u_even)
        normalize_into(u_odd)

    @pl.when((i > 0) & (i % 2 == 0))
    def _():
        project(u_odd)
        normalize_into(u_even)


def _norm_proj(x2d, nw_row, w, layer, *, bm=2048, bn=512):
    n, d = x2d.shape
    in_w = w.shape[2]
    bm = _pick_block(n, bm, 8)
    bn = _pick_block(in_w, bn, 128)
    n_i, n_j = n // bm, in_w // bn
    strip = bm // n_j
    assert strip * n_j == bm and strip % 8 == 0, (bm, n_j)
    last_strip = n // strip - 1
    return pl.pallas_call(
        _norm_proj_kernel,
        grid=(n_i + 1, n_j),
        in_specs=[
            pl.BlockSpec((strip, d), lambda i, j: (jnp.minimum(i * n_j + j, last_strip), 0)),
            pl.BlockSpec((1, d), lambda i, j: (0, 0)),
            pl.BlockSpec((None, d, bn), lambda i, j: (layer, 0, jnp.where(i == 0, 0, j))),
        ],
        out_specs=pl.BlockSpec((bn // HEAD_DIM, bm, HEAD_DIM),
                               lambda i, j: (jnp.where(i == 0, 0, j), jnp.maximum(i - 1, 0), 0)),
        out_shape=jax.ShapeDtypeStruct((in_w // HEAD_DIM, n, HEAD_DIM), BF16),
        scratch_shapes=[pltpu.VMEM((bm, d), BF16), pltpu.VMEM((bm, d), BF16)],
        compiler_params=_params(2),
        name="norm_in_proj",
    )(x2d, nw_row, w)


def _attn_steps(q_ref, k_ref, v_ref, g_ref, cos_ref, sin_ref, o_ref, kaug_ref, vaug_ref):
    t = q_ref.shape[0]
    nb = t // MOBA_BLOCK
    nbp = max(16, -(-nb // 16) * 16)
    half = ROPE_DIM // 2
    cos = cos_ref[...]
    sin = sin_ref[...]
    lane = lax.broadcasted_iota(jnp.int32, (t, HEAD_DIM), 1)

    src = lax.broadcasted_iota(jnp.int32, (HEAD_DIM, HEAD_DIM), 0)
    dst = lax.broadcasted_iota(jnp.int32, (HEAD_DIM, HEAD_DIM), 1)
    pick = ((dst < half) & (src == dst + half)) | ((dst >= half) & (dst < 2 * half) & (src == dst - half))
    pick = pick.astype(BF16)

    def rope(x_bf16):
        partner = jnp.dot(x_bf16, pick, preferred_element_type=F32)
        return x_bf16.astype(F32) * cos + partner * sin

    k = rope(k_ref[...])
    q = rope(q_ref[...])
    q_s = (q * float(HEAD_DIM ** -0.5 * np.log2(np.e))).astype(BF16)

    blk_of_key = lax.broadcasted_iota(jnp.int32, (t, HEAD_DIM), 0) // MOBA_BLOCK
    kaug_ref[:, :HEAD_DIM] = k.astype(BF16)
    kaug_ref[:, HEAD_DIM:] = (blk_of_key == lane).astype(BF16)
    vaug_ref[:, :HEAD_DIM] = v_ref[...]
    vaug_ref[:, HEAD_DIM:] = jnp.ones((t, HEAD_DIM), BF16)

    k_mean = jnp.mean(k.reshape(nb, MOBA_BLOCK, HEAD_DIM), axis=1)
    k_mean = jnp.concatenate([k_mean, jnp.zeros((nbp - nb, HEAD_DIM), F32)], axis=0).astype(BF16)
    gate_t = lax.dot_general(k_mean, q_s, (((1,), (1,)), ((), ())), preferred_element_type=F32)

    row_blk = lax.broadcasted_iota(jnp.int32, (nbp, MOBA_BLOCK), 0)
    q_idx = lax.broadcasted_iota(jnp.int32, (MOBA_BLOCK, MOBA_BLOCK), 0)
    k_idx = lax.broadcasted_iota(jnp.int32, (MOBA_BLOCK, MOBA_BLOCK), 1)
    causal = k_idx <= q_idx

    def masked_scores(qb):
        rows = slice(qb * MOBA_BLOCK, (qb + 1) * MOBA_BLOCK)
        nk = (qb + 1) * MOBA_BLOCK
        q_blk = q_s[rows]
        if qb > MOBA_TOPK:
            g = gate_t[:, rows]
            rank = jnp.zeros((nbp, MOBA_BLOCK), F32)
            for m in range(qb):
                gm = g[m:m + 1, :]
                beats = (gm > g) | ((gm == g) & (m < row_blk))
                rank = rank + beats.astype(F32)
            keep = ((rank < MOBA_TOPK) & (row_blk < qb)) | (row_blk == qb)
            bias_t = jnp.where(keep, 0.0, NEG_INF)
            bias_t = jnp.concatenate([bias_t, jnp.zeros((HEAD_DIM - nbp, MOBA_BLOCK), F32)], axis=0)
            q_aug = jnp.concatenate([q_blk, bias_t.T.astype(BF16)], axis=1)
            s = lax.dot_general(q_aug, kaug_ref[:nk, :], (((1,), (1,)), ((), ())),
                                preferred_element_type=F32)
        else:
            s = lax.dot_general(q_blk, kaug_ref[:nk, :HEAD_DIM], (((1,), (1,)), ((), ())),
                                preferred_element_type=F32)
        s_own = jnp.where(causal, s[:, qb * MOBA_BLOCK:], NEG_INF)
        if qb > 0:
            return jnp.concatenate([s[:, :qb * MOBA_BLOCK], s_own], axis=1)
        return s_own

    def finish(qb, p):
        rows = slice(qb * MOBA_BLOCK, (qb + 1) * MOBA_BLOCK)
        oa = jnp.dot(p, vaug_ref[:(qb + 1) * MOBA_BLOCK, :], preferred_element_type=F32)
        o = oa[:, :HEAD_DIM] / oa[:, HEAD_DIM:]
        o_ref[rows, :] = (o * _silu(g_ref[rows, :].astype(F32))).astype(o_ref.dtype)

    s_cur = masked_scores(0)
    p_prev = None
    yield
    for qb in range(nb):
        s_next = masked_scores(qb + 1) if qb + 1 < nb else None
        if p_prev is not None:
            finish(qb - 1, p_prev)
        m_row = jnp.max(s_cur, axis=-1, keepdims=True)
        p_prev = jnp.exp2(s_cur - m_row).astype(BF16)
        s_cur = s_next
        yield
    finish(nb - 1, p_prev)
    yield


def _rec_steps(layer, q_ref, z_ref, v_ref, g_ref, lbl_ref, nw_ref, o_ref,
               qin_ref, rhs_ref, upd_ref, cum_ref):
    t = q_ref.shape[0]
    n_chunks = t // REC_CHUNK

    logits = lbl_ref[...].astype(F32)
    e = jnp.exp(logits - jnp.max(logits, axis=0, keepdims=True))
    probs = e / jnp.sum(e, axis=0, keepdims=True)
    lb = jnp.sum(probs[:layer + 1], axis=0, keepdims=True)

    half_t = 0.5 * jnp.tanh(0.5 * z_ref[...].astype(F32))
    log2_f = jnp.log2(lb + (1.0 - lb) * (0.5 + half_t))
    k_r = (1.0 - lb) * (0.5 - half_t)
    q_r = _silu(q_ref[...].astype(F32)) * (HEAD_DIM ** -0.5)

    pos = lax.broadcasted_iota(jnp.int32, (t, HEAD_DIM), 0) % REC_CHUNK
    cum = log2_f
    shift = 1
    while shift < REC_CHUNK:
        cum = cum + jnp.where(pos >= shift, pltpu.roll(cum, shift, 0), 0.0)
        shift *= 2
    cum_ref[...] = cum

    mid = REC_CHUNK // 2 - 1
    cum3 = cum.reshape(n_chunks, REC_CHUNK, HEAD_DIM)
    rel = cum3 - cum3[:, mid:mid + 1, :]
    qin_ref[...] = (q_r.reshape(cum3.shape) * jnp.exp2(rel)).reshape(t, HEAD_DIM).astype(BF16)
    rhs_ref[:, HEAD_DIM:, :] = (k_r.reshape(cum3.shape) * jnp.exp2(-rel)).astype(BF16)
    yield

    def chunk_rows(c):
        return slice(c * REC_CHUNK, (c + 1) * REC_CHUNK)

    for c in range(n_chunks):
        upd_ref[c] = lax.dot_general(v_ref[chunk_rows(c), :], rhs_ref[c, HEAD_DIM:, :],
                                     (((0,), (0,)), ((), ())), preferred_element_type=F32)
        if (c + 1) % REC_GROUP == 0:
            yield

    state_t = jnp.zeros((HEAD_DIM, HEAD_DIM), F32)
    for c in range(n_chunks):
        c_mid = cum_ref[c * REC_CHUNK + mid:c * REC_CHUNK + mid + 1, :]
        c_last = cum_ref[(c + 1) * REC_CHUNK - 1:(c + 1) * REC_CHUNK, :]
        rhs_ref[c, :HEAD_DIM, :] = (state_t * jnp.exp2(c_mid)).astype(BF16)
        state_t = state_t * jnp.exp2(c_last) + upd_ref[c] * jnp.exp2(c_last - c_mid)
        if (c + 1) % REC_GROUP == 0:
            yield

    ri = lax.broadcasted_iota(jnp.int32, (REC_CHUNK, REC_CHUNK), 0)
    ci = lax.broadcasted_iota(jnp.int32, (REC_CHUNK, REC_CHUNK), 1)
    tril = ri >= ci
    nw = nw_ref[...].astype(F32)
    for g0 in range(0, n_chunks, REC_GROUP):
        group = range(g0, min(g0 + REC_GROUP, n_chunks))
        both = [lax.dot_general(qin_ref[chunk_rows(c), :], rhs_ref[c], (((1,), (1,)), ((), ())),
                                preferred_element_type=F32) for c in group]
        for c, b in zip(group, both):
            rows = chunk_rows(c)
            scores = jnp.where(tril, b[:, HEAD_DIM:], 0.0).astype(BF16)
            o = b[:, :HEAD_DIM] + jnp.dot(scores, v_ref[rows, :], preferred_element_type=F32)
            o = o * lax.rsqrt(jnp.mean(o * o, axis=-1, keepdims=True) + NORM_EPS)
            o = o * nw * _silu(g_ref[rows, :].astype(F32))
            o_ref[rows, :] = o.astype(o_ref.dtype)
        yield


def _mixer_kernel(layer, aq, ak, av, ag, cos, sin, rq, rz, rv, rg, lbl, nw, attn_o, rec_o,
                  kaug, vaug, qin, rhs, upd, cum):
    streams = [_attn_steps(aq, ak, av, ag, cos, sin, attn_o, kaug, vaug),
               _rec_steps(layer, rq, rz, rv, rg, lbl, nw, rec_o, qin, rhs, upd, cum)]
    stop = object()
    while streams:
        streams = [g for g in streams if next(g, stop) is not stop]


def _mixers(proj, lbl, nw_row, cos_t, sin_t, *, layer, batch, seq, n_heads):
    n = batch * seq
    spec = lambda off: pl.BlockSpec((None, seq, HEAD_DIM), lambda b, h: (off + h, b, 0))
    tbl = pl.BlockSpec((seq, HEAD_DIM), lambda b, h: (0, 0))
    n_chunks = seq // REC_CHUNK
    out = jax.ShapeDtypeStruct((n_heads, n, HEAD_DIM), BF16)
    return pl.pallas_call(
        functools.partial(_mixer_kernel, layer),
        grid=(batch, n_heads),
        in_specs=[
            spec(0), spec(n_heads), spec(2 * n_heads), spec(3 * n_heads), tbl, tbl,
            spec(4 * n_heads), spec(5 * n_heads), spec(6 * n_heads), spec(7 * n_heads),
            pl.BlockSpec((lbl.shape[0], HEAD_DIM), lambda b, h: (0, h)),
            pl.BlockSpec((1, HEAD_DIM), lambda b, h: (0, h)),
        ],
        out_specs=[pl.BlockSpec((None, seq, HEAD_DIM), lambda b, h: (h, b, 0))] * 2,
        out_shape=[out, out],
        scratch_shapes=[
            pltpu.VMEM((seq, 2 * HEAD_DIM), BF16),
            pltpu.VMEM((seq, 2 * HEAD_DIM), BF16),
            pltpu.VMEM((seq, HEAD_DIM), BF16),
            pltpu.VMEM((n_chunks, HEAD_DIM + REC_CHUNK, HEAD_DIM), BF16),
            pltpu.VMEM((n_chunks, HEAD_DIM, HEAD_DIM), F32),
            pltpu.VMEM((seq, HEAD_DIM), F32),
        ],
        compiler_params=_params(2),
        name="mixers",
    )(proj, proj, proj, proj, cos_t, sin_t, proj, proj, proj, proj, lbl, nw_row)


def _out_kernel(final_norm, n_attn_k, n_k, attn_ref, rec_ref, w_ref, x_ref, nw_ref, o_ref):
    kk = pl.program_id(1)
    cw = x_ref.shape[1]

    def partial_product():
        heads = [jnp.where(kk < n_attn_k, attn_ref[c], rec_ref[c]) for c in range(attn_ref.shape[0])]
        return jnp.dot(jnp.concatenate(heads, axis=1), w_ref[...], preferred_element_type=F32)

    def add_residual(j):
        o_ref[:, j * cw:(j + 1) * cw] += x_ref[...]

    def finish():
        if final_norm:
            h = o_ref[...]
            ms = jnp.mean(h * h, axis=-1, keepdims=True)
            o_ref[...] = h * lax.rsqrt(ms + NORM_EPS) * nw_ref[...]

    if n_k == 1:
        o_ref[...] = partial_product()
        add_residual(0)
        finish()
        return

    @pl.when(kk == 0)
    def _():
        o_ref[...] = partial_product()
        add_residual(0)

    @pl.when((kk > 0) & (kk < n_k - 1))
    def _():
        o_ref[...] += partial_product()

    for j in range(1, n_k - 1):
        @pl.when(kk == j)
        def _(j=j):
            add_residual(j)

    @pl.when(kk == n_k - 1)
    def _():
        o_ref[...] += partial_product()
        add_residual(n_k - 1)
        finish()


def _out_proj(attn, rec, w_bf16, x2d, nw_row, *, final_norm, bm=1024, bk=512):
    n, d = x2d.shape
    a_w = attn.shape[0] * HEAD_DIM
    r_w = rec.shape[0] * HEAD_DIM
    bm = _pick_block(n, bm, 8)
    bk = _pick_block(int(np.gcd(a_w, r_w)), bk, 128)
    n_attn_k = a_w // bk
    n_k = (a_w + r_w) // bk
    assert d % (n_k * 128) == 0
    return pl.pallas_call(
        functools.partial(_out_kernel, final_norm, n_attn_k, n_k),
        grid=(n // bm, n_k),
        in_specs=[
            pl.BlockSpec((bk // HEAD_DIM, bm, HEAD_DIM), lambda i, k: (jnp.minimum(k, n_attn_k - 1), i, 0)),
            pl.BlockSpec((bk // HEAD_DIM, bm, HEAD_DIM), lambda i, k: (jnp.maximum(k - n_attn_k, 0), i, 0)),
            pl.BlockSpec((bk, d), lambda i, k: (k, 0)),
            pl.BlockSpec((bm, d // n_k), lambda i, k: (i, k)),
            pl.BlockSpec((1, d), lambda i, k: (0, 0)),
        ],
        out_specs=pl.BlockSpec((bm, d), lambda i, k: (i, 0)),
        out_shape=jax.ShapeDtypeStruct((n, d), F32),
        compiler_params=_params(2),
        name="out_proj",
    )(attn, rec, w_bf16, x2d, nw_row)


def _rope_tables(seq):
    half = ROPE_DIM // 2
    inv_freq = jnp.power(ROPE_THETA, -jnp.arange(half, dtype=F32) / half)
    ang = jnp.arange(seq, dtype=F32)[:, None] * inv_freq[None, :]
    cos, sin = jnp.cos(ang), jnp.sin(ang)
    rest = HEAD_DIM - ROPE_DIM
    cos_t = jnp.concatenate([cos, cos, jnp.ones((seq, rest), F32)], axis=1)
    sin_t = jnp.concatenate([-sin, sin, jnp.zeros((seq, rest), F32)], axis=1)
    return cos_t, sin_t


def kernel(x, norm_w, w_in, rec_lower_bound_logits, rec_out_norm_w, w_out, final_norm_w):
    batch, seq, d_model = x.shape
    depth = norm_w.shape[0]
    mix = w_out.shape[1]
    attn_w = mix // 2
    rec_w = mix - attn_w
    n_ah = attn_w // HEAD_DIM
    n_rh = rec_w // HEAD_DIM
    key_w = n_rh * HEAD_DIM
    assert seq % MOBA_BLOCK == 0 and seq % REC_CHUNK == 0
    assert n_ah == n_rh
    assert w_in.shape[2] == 4 * attn_w + 2 * key_w + 2 * rec_w

    cos_t, sin_t = _rope_tables(seq)
    h = x.reshape(batch * seq, d_model)
    for layer in range(depth):
        proj = _norm_proj(h, norm_w[layer].reshape(1, d_model), w_in, layer)
        rec_nw = rec_out_norm_w[layer].reshape(1, rec_w)
        attn, rec = _mixers(proj, rec_lower_bound_logits, rec_nw, cos_t, sin_t,
                            layer=layer, batch=batch, seq=seq, n_heads=n_ah)
        last = layer == depth - 1
        h = _out_proj(attn, rec, w_out[layer].astype(BF16), h, final_norm_w.reshape(1, d_model),
                      final_norm=last)
    return h.reshape(batch, seq, d_model)
```

```python
import functools

import jax
import jax.numpy as jnp
import numpy as np
from jax import lax
from jax.experimental import pallas as pl
from jax.experimental.pallas import tpu as pltpu

HEAD_DIM = 128
ROPE_THETA = 500000.0
ROPE_DIM = HEAD_DIM // 4
MOBA_BLOCK = 256
MOBA_TOPK = 3
REC_CHUNK = 64
REC_GROUP = 8
NORM_EPS = 1e-6
NEG_INF = -1e30

VMEM_LIMIT_BYTES = 60 * 1024 * 1024

F32 = jnp.float32
BF16 = jnp.bfloat16


def _params(n_grid_dims):
    return pltpu.CompilerParams(
        dimension_semantics=("arbitrary",) * n_grid_dims,
        vmem_limit_bytes=VMEM_LIMIT_BYTES,
    )


def _pick_block(dim, target, align):
    best = None
    for b in range(align, min(dim, target) + 1, align):
        if dim % b == 0:
            best = b
    assert best is not None, (dim, target, align)
    return best


def _sigmoid(x):
    return 0.5 * jnp.tanh(0.5 * x) + 0.5


def _silu(x):
    return x * _sigmoid(x)


def _norm_proj_kernel(x_ref, nw_ref, w_ref, o_ref, u_even, u_odd):
    i = pl.program_id(0)
    j = pl.program_id(1)
    strip = x_ref.shape[0]

    def normalize_into(u_dst):
        x = x_ref[...]
        ms = jnp.mean(x * x, axis=-1, keepdims=True)
        rows = pl.ds(pl.multiple_of(j * strip, strip), strip)
        u_dst[rows, :] = (x * lax.rsqrt(ms + NORM_EPS) * nw_ref[...]).astype(BF16)

    def project(u_src):
        acc = jnp.dot(u_src[...], w_ref[...].astype(BF16), preferred_element_type=F32)
        for c in range(o_ref.shape[0]):
            o_ref[c] = acc[:, c * HEAD_DIM:(c + 1) * HEAD_DIM].astype(o_ref.dtype)

    @pl.when(i == 0)
    def _():
        normalize_into(u_even)

    @pl.when((i > 0) & (i % 2 == 1))
    def _():
        project(u_even)
        normalize_into(u_odd)

    @pl.when((i > 0) & (i % 2 == 0))
    def _():
        project(u_odd)
        normalize_into(u_even)


def _norm_proj(x2d, nw_row, w, layer, *, bm=2048, bn=512):
    n, d = x2d.shape
    in_w = w.shape[2]
    bm = _pick_block(n, bm, 8)
    bn = _pick_block(in_w, bn, 128)
    n_i, n_j = n // bm, in_w // bn
    strip = bm // n_j
    assert strip * n_j == bm and strip % 8 == 0, (bm, n_j)
    last_strip = n // strip - 1
    return pl.pallas_call(
        _norm_proj_kernel,
        grid=(n_i + 1, n_j),
        in_specs=[
            pl.BlockSpec((strip, d), lambda i, j: (jnp.minimum(i * n_j + j, last_strip), 0)),
            pl.BlockSpec((1, d), lambda i, j: (0, 0)),
            pl.BlockSpec((None, d, bn), lambda i, j: (layer, 0, jnp.where(i == 0, 0, j))),
        ],
        out_specs=pl.BlockSpec((bn // HEAD_DIM, bm, HEAD_DIM),
                               lambda i, j: (jnp.where(i == 0, 0, j), jnp.maximum(i - 1, 0), 0)),
        out_shape=jax.ShapeDtypeStruct((in_w // HEAD_DIM, n, HEAD_DIM), BF16),
        scratch_shapes=[pltpu.VMEM((bm, d), BF16), pltpu.VMEM((bm, d), BF16)],
        compiler_params=_params(2),
        name="norm_in_proj",
    )(x2d, nw_row, w)


def _attn_steps(q_ref, k_ref, v_ref, g_ref, cos_ref, sin_ref, o_ref, kaug_ref, vaug_ref, qs_ref):
    t = q_ref.shape[0]
    nb = t // MOBA_BLOCK
    nbp = max(16, -(-nb // 16) * 16)
    half = ROPE_DIM // 2

    src = lax.broadcasted_iota(jnp.int32, (HEAD_DIM, HEAD_DIM), 0)
    dst = lax.broadcasted_iota(jnp.int32, (HEAD_DIM, HEAD_DIM), 1)
    pick = ((dst < half) & (src == dst + half)) | ((dst >= half) & (dst < 2 * half) & (src == dst - half))
    pick = pick.astype(BF16)
    lane = lax.broadcasted_iota(jnp.int32, (MOBA_BLOCK, HEAD_DIM), 1)

    k_means = []
    for blk in range(nb):
        rows = slice(blk * MOBA_BLOCK, (blk + 1) * MOBA_BLOCK)
        cos = cos_ref[rows, :]
        sin = sin_ref[rows, :]

        def rope(x_bf16):
            partner = jnp.dot(x_bf16, pick, preferred_element_type=F32)
            return x_bf16.astype(F32) * cos + partner * sin

        k = rope(k_ref[rows, :])
        kaug_ref[rows, :HEAD_DIM] = k.astype(BF16)
        kaug_ref[rows, HEAD_DIM:] = (lane == blk).astype(BF16)
        k_means.append(jnp.mean(k, axis=0, keepdims=True))
        qs_ref[rows, :] = (rope(q_ref[rows, :]) * float(HEAD_DIM ** -0.5 * np.log2(np.e))).astype(BF16)
        vaug_ref[rows, :HEAD_DIM] = v_ref[rows, :]
        vaug_ref[rows, HEAD_DIM:] = jnp.ones((MOBA_BLOCK, HEAD_DIM), BF16)

    k_mean = jnp.concatenate(k_means + [jnp.zeros((nbp - nb, HEAD_DIM), F32)], axis=0).astype(BF16)
    gate_t = lax.dot_general(k_mean, qs_ref[...], (((1,), (1,)), ((), ())),
                             preferred_element_type=F32)

    row_blk = lax.broadcasted_iota(jnp.int32, (nbp, MOBA_BLOCK), 0)
    q_idx = lax.broadcasted_iota(jnp.int32, (MOBA_BLOCK, MOBA_BLOCK), 0)
    k_idx = lax.broadcasted_iota(jnp.int32, (MOBA_BLOCK, MOBA_BLOCK), 1)
    causal = k_idx <= q_idx

    def masked_scores(qb):
        rows = slice(qb * MOBA_BLOCK, (qb + 1) * MOBA_BLOCK)
        nk = (qb + 1) * MOBA_BLOCK
        q_blk = qs_ref[rows, :]
        if qb > MOBA_TOPK:
            g = gate_t[:, rows]
            rank = jnp.zeros((nbp, MOBA_BLOCK), F32)
            for m in range(qb):
                gm = g[m:m + 1, :]
                beats = (gm > g) | ((gm == g) & (m < row_blk))
                rank = rank + beats.astype(F32)
            keep = ((rank < MOBA_TOPK) & (row_blk < qb)) | (row_blk == qb)
            bias_t = jnp.where(keep, 0.0, NEG_INF)
            bias_t = jnp.concatenate([bias_t, jnp.zeros((HEAD_DIM - nbp, MOBA_BLOCK), F32)], axis=0)
            q_aug = jnp.concatenate([q_blk, bias_t.T.astype(BF16)], axis=1)
            s = lax.dot_general(q_aug, kaug_ref[:nk, :], (((1,), (1,)), ((), ())),
                                preferred_element_type=F32)
        else:
            s = lax.dot_general(q_blk, kaug_ref[:nk, :HEAD_DIM], (((1,), (1,)), ((), ())),
                                preferred_element_type=F32)
        s_own = jnp.where(causal, s[:, qb * MOBA_BLOCK:], NEG_INF)
        if qb > 0:
            return jnp.concatenate([s[:, :qb * MOBA_BLOCK], s_own], axis=1)
        return s_own

    def finish(qb, p):
        rows = slice(qb * MOBA_BLOCK, (qb + 1) * MOBA_BLOCK)
        oa = jnp.dot(p, vaug_ref[:(qb + 1) * MOBA_BLOCK, :], preferred_element_type=F32)
        o = oa[:, :HEAD_DIM] / oa[:, HEAD_DIM:]
        o_ref[rows, :] = (o * _silu(g_ref[rows, :].astype(F32))).astype(o_ref.dtype)

    s_cur = masked_scores(0)
    p_prev = None
    yield
    for qb in range(nb):
        s_next = masked_scores(qb + 1) if qb + 1 < nb else None
        if p_prev is not None:
            finish(qb - 1, p_prev)
        m_row = jnp.max(s_cur, axis=-1, keepdims=True)
        p_prev = jnp.exp2(s_cur - m_row).astype(BF16)
        s_cur = s_next
        yield
    finish(nb - 1, p_prev)
    yield


def _rec_steps(layer, q_ref, z_ref, v_ref, g_ref, lbl_ref, nw_ref, o_ref,
               qin_ref, rhs_ref, upd_ref, cum_ref):
    t = q_ref.shape[0]
    n_chunks = t // REC_CHUNK

    logits = lbl_ref[...].astype(F32)
    e = jnp.exp(logits - jnp.max(logits, axis=0, keepdims=True))
    probs = e / jnp.sum(e, axis=0, keepdims=True)
    lb = jnp.sum(probs[:layer + 1], axis=0, keepdims=True)

    def chunk_rows(c):
        return slice(c * REC_CHUNK, (c + 1) * REC_CHUNK)

    mid = REC_CHUNK // 2 - 1
    pos = lax.broadcasted_iota(jnp.int32, (REC_CHUNK, HEAD_DIM), 0)
    for c in range(n_chunks):
        rows = chunk_rows(c)
        half_t = 0.5 * jnp.tanh(0.5 * z_ref[rows, :].astype(F32))
        cum = jnp.log2(lb + (1.0 - lb) * (0.5 + half_t))
        k_r = (1.0 - lb) * (0.5 - half_t)
        q_r = _silu(q_ref[rows, :].astype(F32)) * (HEAD_DIM ** -0.5)
        shift = 1
        while shift < REC_CHUNK:
            cum = cum + jnp.where(pos >= shift, pltpu.roll(cum, shift, 0), 0.0)
            shift *= 2
        cum_ref[rows, :] = cum
        rel = cum - cum[mid:mid + 1, :]
        qin_ref[rows, :] = (q_r * jnp.exp2(rel)).astype(BF16)
        rhs_ref[c, HEAD_DIM:, :] = (k_r * jnp.exp2(-rel)).astype(BF16)
        if (c + 1) % REC_GROUP == 0:
            yield

    for c in range(n_chunks):
        upd_ref[c] = lax.dot_general(v_ref[chunk_rows(c), :], rhs_ref[c, HEAD_DIM:, :],
                                     (((0,), (0,)), ((), ())), preferred_element_type=F32)
        if (c + 1) % REC_GROUP == 0:
            yield

    state_t = jnp.zeros((HEAD_DIM, HEAD_DIM), F32)
    for c in range(n_chunks):
        c_mid = cum_ref[c * REC_CHUNK + mid:c * REC_CHUNK + mid + 1, :]
        c_last = cum_ref[(c + 1) * REC_CHUNK - 1:(c + 1) * REC_CHUNK, :]
        rhs_ref[c, :HEAD_DIM, :] = (state_t * jnp.exp2(c_mid)).astype(BF16)
        state_t = state_t * jnp.exp2(c_last) + upd_ref[c] * jnp.exp2(c_last - c_mid)
        if (c + 1) % REC_GROUP == 0:
            yield

    ri = lax.broadcasted_iota(jnp.int32, (REC_CHUNK, REC_CHUNK), 0)
    ci = lax.broadcasted_iota(jnp.int32, (REC_CHUNK, REC_CHUNK), 1)
    tril = ri >= ci
    nw = nw_ref[...].astype(F32)
    for g0 in range(0, n_chunks, REC_GROUP):
        group = range(g0, min(g0 + REC_GROUP, n_chunks))
        both = [lax.dot_general(qin_ref[chunk_rows(c), :], rhs_ref[c], (((1,), (1,)), ((), ())),
                                preferred_element_type=F32) for c in group]
        for c, b in zip(group, both):
            rows = chunk_rows(c)
            scores = jnp.where(tril, b[:, HEAD_DIM:], 0.0).astype(BF16)
            o = b[:, :HEAD_DIM] + jnp.dot(scores, v_ref[rows, :], preferred_element_type=F32)
            o = o * lax.rsqrt(jnp.mean(o * o, axis=-1, keepdims=True) + NORM_EPS)
            o = o * nw * _silu(g_ref[rows, :].astype(F32))
            o_ref[rows, :] = o.astype(o_ref.dtype)
        yield


def _mixer_kernel(layer, aq, ak, av, ag, cos, sin, rq, rz, rv, rg, lbl, nw, wo, attn_o, rec_o, wo_bf16,
                  kaug, vaug, qs, qin, rhs, upd, cum):
    wo_bf16[...] = wo[...].astype(BF16)
    streams = [_attn_steps(aq, ak, av, ag, cos, sin, attn_o, kaug, vaug, qs),
               _rec_steps(layer, rq, rz, rv, rg, lbl, nw, rec_o, qin, rhs, upd, cum)]
    stop = object()
    while streams:
        streams = [g for g in streams if next(g, stop) is not stop]


def _mixers(proj, lbl, nw_row, cos_t, sin_t, w_out, *, layer, batch, seq, n_heads):
    n = batch * seq
    mix, d = w_out.shape[1:]
    wo_rows = mix // (batch * n_heads)
    assert wo_rows * batch * n_heads == mix and wo_rows % 16 == 0
    spec = lambda off: pl.BlockSpec((None, seq, HEAD_DIM), lambda b, h: (off + h, b, 0))
    tbl = pl.BlockSpec((seq, HEAD_DIM), lambda b, h: (0, 0))
    n_chunks = seq // REC_CHUNK
    out = jax.ShapeDtypeStruct((n_heads, n, HEAD_DIM), BF16)
    return pl.pallas_call(
        functools.partial(_mixer_kernel, layer),
        grid=(batch, n_heads),
        in_specs=[
            spec(0), spec(n_heads), spec(2 * n_heads), spec(3 * n_heads), tbl, tbl,
            spec(4 * n_heads), spec(5 * n_heads), spec(6 * n_heads), spec(7 * n_heads),
            pl.BlockSpec((lbl.shape[0], HEAD_DIM), lambda b, h: (0, h)),
            pl.BlockSpec((1, HEAD_DIM), lambda b, h: (0, h)),
            pl.BlockSpec((None, wo_rows, d), lambda b, h: (layer, b * n_heads + h, 0)),
        ],
        out_specs=[pl.BlockSpec((None, seq, HEAD_DIM), lambda b, h: (h, b, 0))] * 2
        + [pl.BlockSpec((wo_rows, d), lambda b, h: (b * n_heads + h, 0))],
        out_shape=[out, out, jax.ShapeDtypeStruct((mix, d), BF16)],
        scratch_shapes=[
            pltpu.VMEM((seq, 2 * HEAD_DIM), BF16),
            pltpu.VMEM((seq, 2 * HEAD_DIM), BF16),
            pltpu.VMEM((seq, HEAD_DIM), BF16),
            pltpu.VMEM((seq, HEAD_DIM), BF16),
            pltpu.VMEM((n_chunks, HEAD_DIM + REC_CHUNK, HEAD_DIM), BF16),
            pltpu.VMEM((n_chunks, HEAD_DIM, HEAD_DIM), F32),
            pltpu.VMEM((seq, HEAD_DIM), F32),
        ],
        compiler_params=_params(2),
        name="mixers",
    )(proj, proj, proj, proj, cos_t, sin_t, proj, proj, proj, proj, lbl, nw_row, w_out)


def _out_kernel(final_norm, n_attn_k, n_k, attn_ref, rec_ref, w_ref, x_ref, nw_ref, o_ref):
    kk = pl.program_id(1)
    cw = x_ref.shape[1]

    def partial_product():
        heads = [jnp.where(kk < n_attn_k, attn_ref[c], rec_ref[c]) for c in range(attn_ref.shape[0])]
        return jnp.dot(jnp.concatenate(heads, axis=1), w_ref[...], preferred_element_type=F32)

    def add_residual(j):
        o_ref[:, j * cw:(j + 1) * cw] += x_ref[...]

    def finish():
        if final_norm:
            h = o_ref[...]
            ms = jnp.mean(h * h, axis=-1, keepdims=True)
            o_ref[...] = h * lax.rsqrt(ms + NORM_EPS) * nw_ref[...]

    if n_k == 1:
        o_ref[...] = partial_product()
        add_residual(0)
        finish()
        return

    @pl.when(kk == 0)
    def _():
        o_ref[...] = partial_product()
        add_residual(0)

    @pl.when((kk > 0) & (kk < n_k - 1))
    def _():
        o_ref[...] += partial_product()

    for j in range(1, n_k - 1):
        @pl.when(kk == j)
        def _(j=j):
            add_residual(j)

    @pl.when(kk == n_k - 1)
    def _():
        o_ref[...] += partial_product()
        add_residual(n_k - 1)
        finish()


def _out_proj(attn, rec, w_bf16, x2d, nw_row, *, final_norm, bm=1024, bk=512):
    n, d = x2d.shape
    a_w = attn.shape[0] * HEAD_DIM
    r_w = rec.shape[0] * HEAD_DIM
    bm = _pick_block(n, bm, 8)
    bk = _pick_block(int(np.gcd(a_w, r_w)), bk, 128)
    n_attn_k = a_w // bk
    n_k = (a_w + r_w) // bk
    assert d % (n_k * 128) == 0
    return pl.pallas_call(
        functools.partial(_out_kernel, final_norm, n_attn_k, n_k),
        grid=(n // bm, n_k),
        in_specs=[
            pl.BlockSpec((bk // HEAD_DIM, bm, HEAD_DIM), lambda i, k: (jnp.minimum(k, n_attn_k - 1), i, 0)),
            pl.BlockSpec((bk // HEAD_DIM, bm, HEAD_DIM), lambda i, k: (jnp.maximum(k - n_attn_k, 0), i, 0)),
            pl.BlockSpec((bk, d), lambda i, k: (k, 0)),
            pl.BlockSpec((bm, d // n_k), lambda i, k: (i, k)),
            pl.BlockSpec((1, d), lambda i, k: (0, 0)),
        ],
        out_specs=pl.BlockSpec((bm, d), lambda i, k: (i, 0)),
        out_shape=jax.ShapeDtypeStruct((n, d), F32),
        compiler_params=_params(2),
        name="out_proj",
    )(attn, rec, w_bf16, x2d, nw_row)


def _rope_tables(seq):
    half = ROPE_DIM // 2
    inv_freq = jnp.power(ROPE_THETA, -jnp.arange(half, dtype=F32) / half)
    ang = jnp.arange(seq, dtype=F32)[:, None] * inv_freq[None, :]
    cos, sin = jnp.cos(ang), jnp.sin(ang)
    rest = HEAD_DIM - ROPE_DIM
    cos_t = jnp.concatenate([cos, cos, jnp.ones((seq, rest), F32)], axis=1)
    sin_t = jnp.concatenate([-sin, sin, jnp.zeros((seq, rest), F32)], axis=1)
    return cos_t, sin_t


def kernel(x, norm_w, w_in, rec_lower_bound_logits, rec_out_norm_w, w_out, final_norm_w):
    batch, seq, d_model = x.shape
    depth = norm_w.shape[0]
    mix = w_out.shape[1]
    attn_w = mix // 2
    rec_w = mix - attn_w
    n_ah = attn_w // HEAD_DIM
    n_rh = rec_w // HEAD_DIM
    key_w = n_rh * HEAD_DIM
    assert seq % MOBA_BLOCK == 0 and seq % REC_CHUNK == 0
    assert n_ah == n_rh
    assert w_in.shape[2] == 4 * attn_w + 2 * key_w + 2 * rec_w

    cos_t, sin_t = _rope_tables(seq)
    h = x.reshape(batch * seq, d_model)
    for layer in range(depth):
        proj = _norm_proj(h, norm_w[layer].reshape(1, d_model), w_in, layer)
        rec_nw = rec_out_norm_w[layer].reshape(1, rec_w)
        attn, rec, w_out_bf16 = _mixers(proj, rec_lower_bound_logits, rec_nw, cos_t, sin_t, w_out,
                                        layer=layer, batch=batch, seq=seq, n_heads=n_ah)
        last = layer == depth - 1
        h = _out_proj(attn, rec, w_out_bf16, h, final_norm_w.reshape(1, d_model),
                      final_norm=last)
    return h.reshape(batch, seq, d_model)
```

```python
import functools

import jax
import jax.numpy as jnp
import numpy as np
from jax import lax
from jax.experimental import pallas as pl
from jax.experimental.pallas import tpu as pltpu

HEAD_DIM = 128
ROPE_THETA = 500000.0
ROPE_DIM = HEAD_DIM // 4
MOBA_BLOCK = 256
MOBA_TOPK = 3
REC_CHUNK = 64
REC_GROUP = 8
MIX_HEADS = 2
OUT_NORM_SPLIT = 4
NORM_EPS = 1e-6
NEG_INF = -1e30

VMEM_LIMIT_BYTES = 60 * 1024 * 1024

F32 = jnp.float32
BF16 = jnp.bfloat16


def _params(n_grid_dims):
    return pltpu.CompilerParams(
        dimension_semantics=("arbitrary",) * n_grid_dims,
        vmem_limit_bytes=VMEM_LIMIT_BYTES,
    )


def _pick_block(dim, target, align):
    best = None
    for b in range(align, min(dim, target) + 1, align):
        if dim % b == 0:
            best = b
    assert best is not None, (dim, target, align)
    return best


def _sigmoid(x):
    return 0.5 * jnp.tanh(0.5 * x) + 0.5


def _silu(x):
    return x * _sigmoid(x)


def _norm_proj_kernel(x_ref, nw_ref, w_ref, o_ref, u_even, u_odd):
    i = pl.program_id(0)
    j = pl.program_id(1)
    strip = x_ref.shape[0]

    def normalize_into(u_dst):
        x = x_ref[...]
        ms = jnp.mean(x * x, axis=-1, keepdims=True)
        rows = pl.ds(pl.multiple_of(j * strip, strip), strip)
        u_dst[rows, :] = (x * lax.rsqrt(ms + NORM_EPS) * nw_ref[...]).astype(BF16)

    def project(u_src):
        acc = jnp.dot(u_src[...], w_ref[...].astype(BF16), preferred_element_type=F32)
        for c in range(o_ref.shape[0]):
            o_ref[c] = acc[:, c * HEAD_DIM:(c + 1) * HEAD_DIM].astype(o_ref.dtype)

    @pl.when(i == 0)
    def _():
        normalize_into(u_even)

    @pl.when((i > 0) & (i % 2 == 1))
    def _():
        project(u_even)
        normalize_into(u_odd)

    @pl.when((i > 0) & (i % 2 == 0))
    def _():
        project(u_odd)
        normalize_into(u_even)


def _norm_proj(x2d, nw_row, w, layer, *, bm=2048, bn=512):
    n, d = x2d.shape
    in_w = w.shape[2]
    bm = _pick_block(n, bm, 8)
    bn = _pick_block(in_w, bn, 128)
    n_i, n_j = n // bm, in_w // bn
    strip = bm // n_j
    assert strip * n_j == bm and strip % 8 == 0, (bm, n_j)
    last_strip = n // strip - 1
    return pl.pallas_call(
        _norm_proj_kernel,
        grid=(n_i + 1, n_j),
        in_specs=[
            pl.BlockSpec((strip, d), lambda i, j: (jnp.minimum(i * n_j + j, last_strip), 0)),
            pl.BlockSpec((1, d), lambda i, j: (0, 0)),
            pl.BlockSpec((None, d, bn), lambda i, j: (layer, 0, jnp.where(i == 0, 0, j))),
        ],
        out_specs=pl.BlockSpec((bn // HEAD_DIM, bm, HEAD_DIM),
                               lambda i, j: (jnp.where(i == 0, 0, j), jnp.maximum(i - 1, 0), 0)),
        out_shape=jax.ShapeDtypeStruct((in_w // HEAD_DIM, n, HEAD_DIM), BF16),
        scratch_shapes=[pltpu.VMEM((bm, d), BF16), pltpu.VMEM((bm, d), BF16)],
        compiler_params=_params(2),
        name="norm_in_proj",
    )(x2d, nw_row, w)


def _attn_steps(q_ref, k_ref, v_ref, g_ref, cos_ref, sin_ref, o_ref, kaug_ref, vaug_ref, qs_ref):
    t = q_ref.shape[0]
    nb = t // MOBA_BLOCK
    nbp = max(16, -(-nb // 16) * 16)
    half = ROPE_DIM // 2

    src = lax.broadcasted_iota(jnp.int32, (HEAD_DIM, HEAD_DIM), 0)
    dst = lax.broadcasted_iota(jnp.int32, (HEAD_DIM, HEAD_DIM), 1)
    pick = ((dst < half) & (src == dst + half)) | ((dst >= half) & (dst < 2 * half) & (src == dst - half))
    pick = pick.astype(BF16)
    lane = lax.broadcasted_iota(jnp.int32, (MOBA_BLOCK, HEAD_DIM), 1)

    k_means = []
    for blk in range(nb):
        rows = slice(blk * MOBA_BLOCK, (blk + 1) * MOBA_BLOCK)
        cos = cos_ref[rows, :]
        sin = sin_ref[rows, :]

        def rope(x_bf16):
            partner = jnp.dot(x_bf16, pick, preferred_element_type=F32)
            return x_bf16.astype(F32) * cos + partner * sin

        k = rope(k_ref[rows, :])
        kaug_ref[rows, :HEAD_DIM] = k.astype(BF16)
        kaug_ref[rows, HEAD_DIM:] = (lane == blk).astype(BF16)
        k_means.append(jnp.mean(k, axis=0, keepdims=True))
        qs_ref[rows, :] = (rope(q_ref[rows, :]) * float(HEAD_DIM ** -0.5 * np.log2(np.e))).astype(BF16)
        vaug_ref[rows, :HEAD_DIM] = v_ref[rows, :]
        vaug_ref[rows, HEAD_DIM:] = jnp.ones((MOBA_BLOCK, HEAD_DIM), BF16)

    k_mean = jnp.concatenate(k_means + [jnp.zeros((nbp - nb, HEAD_DIM), F32)], axis=0).astype(BF16)
    gate_t = lax.dot_general(k_mean, qs_ref[...], (((1,), (1,)), ((), ())),
                             preferred_element_type=F32)

    row_blk = lax.broadcasted_iota(jnp.int32, (nbp, MOBA_BLOCK), 0)
    q_idx = lax.broadcasted_iota(jnp.int32, (MOBA_BLOCK, MOBA_BLOCK), 0)
    k_idx = lax.broadcasted_iota(jnp.int32, (MOBA_BLOCK, MOBA_BLOCK), 1)
    causal = k_idx <= q_idx

    def masked_scores(qb):
        rows = slice(qb * MOBA_BLOCK, (qb + 1) * MOBA_BLOCK)
        nk = (qb + 1) * MOBA_BLOCK
        q_blk = qs_ref[rows, :]
        if qb > MOBA_TOPK:
            g = gate_t[:, rows]
            rank = jnp.zeros((nbp, MOBA_BLOCK), F32)
            for m in range(qb):
                gm = g[m:m + 1, :]
                beats = (gm > g) | ((gm == g) & (m < row_blk))
                rank = rank + beats.astype(F32)
            keep = ((rank < MOBA_TOPK) & (row_blk < qb)) | (row_blk == qb)
            bias_t = jnp.where(keep, 0.0, NEG_INF)
            bias_t = jnp.concatenate([bias_t, jnp.zeros((HEAD_DIM - nbp, MOBA_BLOCK), F32)], axis=0)
            q_aug = jnp.concatenate([q_blk, bias_t.T.astype(BF16)], axis=1)
            s = lax.dot_general(q_aug, kaug_ref[:nk, :], (((1,), (1,)), ((), ())),
                                preferred_element_type=F32)
        else:
            s = lax.dot_general(q_blk, kaug_ref[:nk, :HEAD_DIM], (((1,), (1,)), ((), ())),
                                preferred_element_type=F32)
        s_own = jnp.where(causal, s[:, qb * MOBA_BLOCK:], NEG_INF)
        if qb > 0:
            return jnp.concatenate([s[:, :qb * MOBA_BLOCK], s_own], axis=1)
        return s_own

    def finish(qb, p):
        rows = slice(qb * MOBA_BLOCK, (qb + 1) * MOBA_BLOCK)
        oa = jnp.dot(p, vaug_ref[:(qb + 1) * MOBA_BLOCK, :], preferred_element_type=F32)
        o = oa[:, :HEAD_DIM] / oa[:, HEAD_DIM:]
        o_ref[rows, :] = (o * _silu(g_ref[rows, :].astype(F32))).astype(o_ref.dtype)

    s_cur = masked_scores(0)
    p_prev = None
    yield
    for qb in range(nb):
        s_next = masked_scores(qb + 1) if qb + 1 < nb else None
        if p_prev is not None:
            finish(qb - 1, p_prev)
        m_row = jnp.max(s_cur, axis=-1, keepdims=True)
        p_prev = jnp.exp2(s_cur - m_row).astype(BF16)
        s_cur = s_next
        yield
    finish(nb - 1, p_prev)
    yield


def _rec_steps(layer, q_ref, z_ref, v_ref, g_ref, lbl_ref, nw_ref, o_ref,
               qin_ref, rhs_ref, upd_ref, cum_ref):
    t = q_ref.shape[0]
    n_chunks = t // REC_CHUNK

    logits = lbl_ref[...].astype(F32)
    e = jnp.exp(logits - jnp.max(logits, axis=0, keepdims=True))
    probs = e / jnp.sum(e, axis=0, keepdims=True)
    lb = jnp.sum(probs[:layer + 1], axis=0, keepdims=True)

    def chunk_rows(c):
        return slice(c * REC_CHUNK, (c + 1) * REC_CHUNK)

    mid = REC_CHUNK // 2 - 1
    tri = (lax.broadcasted_iota(jnp.int32, (REC_CHUNK, 2 * REC_CHUNK), 0)
           >= lax.broadcasted_iota(jnp.int32, (REC_CHUNK, 2 * REC_CHUNK), 1) % REC_CHUNK).astype(BF16)
    for c in range(n_chunks):
        rows = chunk_rows(c)
        half_t = 0.5 * jnp.tanh(0.5 * z_ref[rows, :].astype(F32))
        log2_f = jnp.log2(lb + (1.0 - lb) * (0.5 + half_t))
        k_r = (1.0 - lb) * (0.5 - half_t)
        q_r = _silu(q_ref[rows, :].astype(F32)) * (HEAD_DIM ** -0.5)
        high = log2_f.astype(BF16)
        low = (log2_f - high.astype(F32)).astype(BF16)
        cum = jnp.dot(tri, jnp.concatenate([high, low], axis=0), preferred_element_type=F32)
        cum_ref[rows, :] = cum
        rel = cum - cum[mid:mid + 1, :]
        qin_ref[rows, :] = (q_r * jnp.exp2(rel)).astype(BF16)
        rhs_ref[c, HEAD_DIM:, :] = (k_r * jnp.exp2(-rel)).astype(BF16)
        if (c + 1) % REC_GROUP == 0:
            yield

    for c in range(n_chunks):
        upd_ref[c] = lax.dot_general(v_ref[chunk_rows(c), :], rhs_ref[c, HEAD_DIM:, :],
                                     (((0,), (0,)), ((), ())), preferred_element_type=F32)
        if (c + 1) % REC_GROUP == 0:
            yield

    state_t = jnp.zeros((HEAD_DIM, HEAD_DIM), F32)
    for c in range(n_chunks):
        c_mid = cum_ref[c * REC_CHUNK + mid:c * REC_CHUNK + mid + 1, :]
        c_last = cum_ref[(c + 1) * REC_CHUNK - 1:(c + 1) * REC_CHUNK, :]
        rhs_ref[c, :HEAD_DIM, :] = (state_t * jnp.exp2(c_mid)).astype(BF16)
        state_t = state_t * jnp.exp2(c_last) + upd_ref[c] * jnp.exp2(c_last - c_mid)
        if (c + 1) % REC_GROUP == 0:
            yield

    ri = lax.broadcasted_iota(jnp.int32, (REC_CHUNK, REC_CHUNK), 0)
    ci = lax.broadcasted_iota(jnp.int32, (REC_CHUNK, REC_CHUNK), 1)
    tril = ri >= ci
    nw = nw_ref[...].astype(F32)
    for g0 in range(0, n_chunks, REC_GROUP):
        group = range(g0, min(g0 + REC_GROUP, n_chunks))
        both = [lax.dot_general(qin_ref[chunk_rows(c), :], rhs_ref[c], (((1,), (1,)), ((), ())),
                                preferred_element_type=F32) for c in group]
        for c, b in zip(group, both):
            rows = chunk_rows(c)
            scores = jnp.where(tril, b[:, HEAD_DIM:], 0.0).astype(BF16)
            o = b[:, :HEAD_DIM] + jnp.dot(scores, v_ref[rows, :], preferred_element_type=F32)
            o = o * lax.rsqrt(jnp.mean(o * o, axis=-1, keepdims=True) + NORM_EPS)
            o = o * nw * _silu(g_ref[rows, :].astype(F32))
            o_ref[rows, :] = o.astype(o_ref.dtype)
        yield


def _mixer_kernel(layer, aq, ak, av, ag, cos, sin, rq, rz, rv, rg, lbl, nw, wo, attn_o, rec_o, wo_bf16,
                  kaug, vaug, qs, qin, rhs, upd, cum):
    wo_bf16[...] = wo[...].astype(BF16)
    streams = []
    for i in range(aq.shape[0]):
        lanes = slice(i * HEAD_DIM, (i + 1) * HEAD_DIM)
        streams.append(_attn_steps(aq.at[i], ak.at[i], av.at[i], ag.at[i], cos, sin, attn_o.at[i],
                                   kaug.at[i], vaug.at[i], qs.at[i]))
        streams.append(_rec_steps(layer, rq.at[i], rz.at[i], rv.at[i], rg.at[i], lbl.at[:, lanes],
                                  nw.at[:, lanes], rec_o.at[i], qin.at[i], rhs.at[i], upd.at[i], cum.at[i]))
    stop = object()
    while streams:
        streams = [g for g in streams if next(g, stop) is not stop]


def _mixers(proj, lbl, nw_row, cos_t, sin_t, w_out, *, layer, batch, seq, n_heads):
    n = batch * seq
    mix, d = w_out.shape[1:]
    hp = MIX_HEADS if n_heads % MIX_HEADS == 0 else 1
    n_steps = batch * (n_heads // hp)
    wo_rows = mix // n_steps
    assert wo_rows * n_steps == mix and wo_rows % 16 == 0
    spec = lambda off: pl.BlockSpec((hp, seq, HEAD_DIM), lambda b, h: (off // hp + h, b, 0))
    tbl = pl.BlockSpec((seq, HEAD_DIM), lambda b, h: (0, 0))
    n_chunks = seq // REC_CHUNK
    out = jax.ShapeDtypeStruct((n_heads, n, HEAD_DIM), BF16)
    return pl.pallas_call(
        functools.partial(_mixer_kernel, layer),
        grid=(batch, n_heads // hp),
        in_specs=[
            spec(0), spec(n_heads), spec(2 * n_heads), spec(3 * n_heads), tbl, tbl,
            spec(4 * n_heads), spec(5 * n_heads), spec(6 * n_heads), spec(7 * n_heads),
            pl.BlockSpec((lbl.shape[0], hp * HEAD_DIM), lambda b, h: (0, h)),
            pl.BlockSpec((1, hp * HEAD_DIM), lambda b, h: (0, h)),
            pl.BlockSpec((None, wo_rows, d), lambda b, h: (layer, b * (n_heads // hp) + h, 0)),
        ],
        out_specs=[pl.BlockSpec((hp, seq, HEAD_DIM), lambda b, h: (h, b, 0))] * 2
        + [pl.BlockSpec((wo_rows, d), lambda b, h: (b * (n_heads // hp) + h, 0))],
        out_shape=[out, out, jax.ShapeDtypeStruct((mix, d), BF16)],
        scratch_shapes=[
            pltpu.VMEM((hp, seq, 2 * HEAD_DIM), BF16),
            pltpu.VMEM((hp, seq, 2 * HEAD_DIM), BF16),
            pltpu.VMEM((hp, seq, HEAD_DIM), BF16),
            pltpu.VMEM((hp, seq, HEAD_DIM), BF16),
            pltpu.VMEM((hp, n_chunks, HEAD_DIM + REC_CHUNK, HEAD_DIM), BF16),
            pltpu.VMEM((hp, n_chunks, HEAD_DIM, HEAD_DIM), F32),
            pltpu.VMEM((hp, seq, HEAD_DIM), F32),
        ],
        compiler_params=_params(2),
        name="mixers",
    )(proj, proj, proj, proj, cos_t, sin_t, proj, proj, proj, proj, lbl, nw_row, w_out)


def _out_kernel(final_norm, n_attn_k, n_k, attn_ref, rec_ref, w_ref, x_ref, nw_ref, o_ref):
    kk = pl.program_id(1)
    cw = x_ref.shape[1]

    def partial_product(rows=slice(None)):
        heads = [jnp.where(kk < n_attn_k, attn_ref[c, rows, :], rec_ref[c, rows, :])
                 for c in range(attn_ref.shape[0])]
        return jnp.dot(jnp.concatenate(heads, axis=1), w_ref[...], preferred_element_type=F32)

    def add_residual(j, rows=slice(None)):
        o_ref[rows, j * cw:(j + 1) * cw] += x_ref[rows, :]

    def finish(rows=slice(None)):
        if final_norm:
            h = o_ref[rows, :]
            ms = jnp.mean(h * h, axis=-1, keepdims=True)
            o_ref[rows, :] = h * lax.rsqrt(ms + NORM_EPS) * nw_ref[...]

    if n_k == 1:
        o_ref[...] = partial_product()
        add_residual(0)
        finish()
        return

    @pl.when(kk == 0)
    def _():
        o_ref[...] = partial_product()
        add_residual(0)

    @pl.when((kk > 0) & (kk < n_k - 1))
    def _():
        o_ref[...] += partial_product()

    for j in range(1, n_k - 1):
        @pl.when(kk == j)
        def _(j=j):
            add_residual(j)

    @pl.when(kk == n_k - 1)
    def _():
        bm = o_ref.shape[0]
        step = bm // OUT_NORM_SPLIT if bm % (OUT_NORM_SPLIT * 8) == 0 else bm
        for r0 in range(0, bm, step):
            rows = slice(r0, r0 + step)
            o_ref[rows, :] += partial_product(rows)
            add_residual(n_k - 1, rows)
            finish(rows)


def _out_proj(attn, rec, w_bf16, x2d, nw_row, *, final_norm, bm=1024, bk=512):
    n, d = x2d.shape
    a_w = attn.shape[0] * HEAD_DIM
    r_w = rec.shape[0] * HEAD_DIM
    bm = _pick_block(n, bm, 8)
    bk = _pick_block(int(np.gcd(a_w, r_w)), bk, 128)
    n_attn_k = a_w // bk
    n_k = (a_w + r_w) // bk
    assert d % (n_k * 128) == 0
    return pl.pallas_call(
        functools.partial(_out_kernel, final_norm, n_attn_k, n_k),
        grid=(n // bm, n_k),
        in_specs=[
            pl.BlockSpec((bk // HEAD_DIM, bm, HEAD_DIM), lambda i, k: (jnp.minimum(k, n_attn_k - 1), i, 0)),
            pl.BlockSpec((bk // HEAD_DIM, bm, HEAD_DIM), lambda i, k: (jnp.maximum(k - n_attn_k, 0), i, 0)),
            pl.BlockSpec((bk, d), lambda i, k: (k, 0)),
            pl.BlockSpec((bm, d // n_k), lambda i, k: (i, k)),
            pl.BlockSpec((1, d), lambda i, k: (0, 0)),
        ],
        out_specs=pl.BlockSpec((bm, d), lambda i, k: (i, 0)),
        out_shape=jax.ShapeDtypeStruct((n, d), F32),
        compiler_params=_params(2),
        name="out_proj",
    )(attn, rec, w_bf16, x2d, nw_row)


def _rope_tables(seq):
    half = ROPE_DIM // 2
    inv_freq = jnp.power(ROPE_THETA, -jnp.arange(half, dtype=F32) / half)
    ang = jnp.arange(seq, dtype=F32)[:, None] * inv_freq[None, :]
    cos, sin = jnp.cos(ang), jnp.sin(ang)
    rest = HEAD_DIM - ROPE_DIM
    cos_t = jnp.concatenate([cos, cos, jnp.ones((seq, rest), F32)], axis=1)
    sin_t = jnp.concatenate([-sin, sin, jnp.zeros((seq, rest), F32)], axis=1)
    return cos_t, sin_t


def kernel(x, norm_w, w_in, rec_lower_bound_logits, rec_out_norm_w, w_out, final_norm_w):
    batch, seq, d_model = x.shape
    depth = norm_w.shape[0]
    mix = w_out.shape[1]
    attn_w = mix // 2
    rec_w = mix - attn_w
    n_ah = attn_w // HEAD_DIM
    n_rh = rec_w // HEAD_DIM
    key_w = n_rh * HEAD_DIM
    assert seq % MOBA_BLOCK == 0 and seq % REC_CHUNK == 0
    assert n_ah == n_rh
    assert w_in.shape[2] == 4 * attn_w + 2 * key_w + 2 * rec_w

    cos_t, sin_t = _rope_tables(seq)
    h = x.reshape(batch * seq, d_model)
    for layer in range(depth):
        proj = _norm_proj(h, norm_w[layer].reshape(1, d_model), w_in, layer)
        rec_nw = rec_out_norm_w[layer].reshape(1, rec_w)
        attn, rec, w_out_bf16 = _mixers(proj, rec_lower_bound_logits, rec_nw, cos_t, sin_t, w_out,
                                        layer=layer, batch=batch, seq=seq, n_heads=n_ah)
        last = layer == depth - 1
        h = _out_proj(attn, rec, w_out_bf16, h, final_norm_w.reshape(1, d_model),
                      final_norm=last)
    return h.reshape(batch, seq, d_model)
```

```python
import functools

import jax
import jax.numpy as jnp
import numpy as np
from jax import lax
from jax.experimental import pallas as pl
from jax.experimental.pallas import tpu as pltpu

HEAD_DIM = 128
ROPE_THETA = 500000.0
ROPE_DIM = HEAD_DIM // 4
MOBA_BLOCK = 256
MOBA_TOPK = 3
REC_CHUNK = 64
REC_GROUP = 8
MIX_HEADS = 2
NORM_EPS = 1e-6
NEG_INF = -1e30

VMEM_LIMIT_BYTES = 60 * 1024 * 1024

F32 = jnp.float32
BF16 = jnp.bfloat16


def _params(n_grid_dims):
    return pltpu.CompilerParams(
        dimension_semantics=("arbitrary",) * n_grid_dims,
        vmem_limit_bytes=VMEM_LIMIT_BYTES,
    )


def _pick_block(dim, target, align):
    best = None
    for b in range(align, min(dim, target) + 1, align):
        if dim % b == 0:
            best = b
    assert best is not None, (dim, target, align)
    return best


def _sigmoid(x):
    return 0.5 * jnp.tanh(0.5 * x) + 0.5


def _silu(x):
    return x * _sigmoid(x)


def _norm_proj_kernel(x_ref, nw_ref, w_ref, o_ref, u_even, u_odd):
    i = pl.program_id(0)
    j = pl.program_id(1)
    strip = x_ref.shape[0]

    def normalize_into(u_dst):
        x = x_ref[...]
        ms = jnp.mean(x * x, axis=-1, keepdims=True)
        rows = pl.ds(pl.multiple_of(j * strip, strip), strip)
        u_dst[rows, :] = (x * lax.rsqrt(ms + NORM_EPS) * nw_ref[...]).astype(BF16)

    def project(u_src):
        acc = jnp.dot(u_src[...], w_ref[...].astype(BF16), preferred_element_type=F32)
        for c in range(o_ref.shape[0]):
            o_ref[c] = acc[:, c * HEAD_DIM:(c + 1) * HEAD_DIM].astype(o_ref.dtype)

    @pl.when(i == 0)
    def _():
        normalize_into(u_even)

    @pl.when((i > 0) & (i % 2 == 1))
    def _():
        project(u_even)
        normalize_into(u_odd)

    @pl.when((i > 0) & (i % 2 == 0))
    def _():
        project(u_odd)
        normalize_into(u_even)


def _norm_proj(x2d, nw_row, w, layer, *, bm=2048, bn=512):
    n, d = x2d.shape
    in_w = w.shape[2]
    bm = _pick_block(n, bm, 8)
    bn = _pick_block(in_w, bn, 128)
    n_i, n_j = n // bm, in_w // bn
    strip = bm // n_j
    assert strip * n_j == bm and strip % 8 == 0, (bm, n_j)
    last_strip = n // strip - 1
    return pl.pallas_call(
        _norm_proj_kernel,
        grid=(n_i + 1, n_j),
        in_specs=[
            pl.BlockSpec((strip, d), lambda i, j: (jnp.minimum(i * n_j + j, last_strip), 0)),
            pl.BlockSpec((1, d), lambda i, j: (0, 0)),
            pl.BlockSpec((None, d, bn), lambda i, j: (layer, 0, jnp.where(i == 0, 0, j))),
        ],
        out_specs=pl.BlockSpec((bn // HEAD_DIM, bm, HEAD_DIM),
                               lambda i, j: (jnp.where(i == 0, 0, j), jnp.maximum(i - 1, 0), 0)),
        out_shape=jax.ShapeDtypeStruct((in_w // HEAD_DIM, n, HEAD_DIM), BF16),
        scratch_shapes=[pltpu.VMEM((bm, d), BF16), pltpu.VMEM((bm, d), BF16)],
        compiler_params=_params(2),
        name="norm_in_proj",
    )(x2d, nw_row, w)


def _attn_steps(q_ref, k_ref, v_ref, g_ref, cos_ref, sin_ref, o_ref, kaug_ref, vaug_ref, qs_ref):
    t = q_ref.shape[0]
    nb = t // MOBA_BLOCK
    nbp = max(16, -(-nb // 16) * 16)
    half = ROPE_DIM // 2

    src = lax.broadcasted_iota(jnp.int32, (HEAD_DIM, HEAD_DIM), 0)
    dst = lax.broadcasted_iota(jnp.int32, (HEAD_DIM, HEAD_DIM), 1)
    pick = ((dst < half) & (src == dst + half)) | ((dst >= half) & (dst < 2 * half) & (src == dst - half))
    pick = pick.astype(BF16)
    lane = lax.broadcasted_iota(jnp.int32, (MOBA_BLOCK, HEAD_DIM), 1)

    k_means = []
    for blk in range(nb):
        rows = slice(blk * MOBA_BLOCK, (blk + 1) * MOBA_BLOCK)
        cos = cos_ref[rows, :]
        sin = sin_ref[rows, :]

        def rope(x_bf16):
            partner = jnp.dot(x_bf16, pick, preferred_element_type=F32)
            return x_bf16.astype(F32) * cos + partner * sin

        k = rope(k_ref[rows, :])
        kaug_ref[rows, :HEAD_DIM] = k.astype(BF16)
        kaug_ref[rows, HEAD_DIM:] = (lane == blk).astype(BF16)
        k_means.append(jnp.mean(k, axis=0, keepdims=True))
        qs_ref[rows, :] = (rope(q_ref[rows, :]) * float(HEAD_DIM ** -0.5 * np.log2(np.e))).astype(BF16)
        vaug_ref[rows, :HEAD_DIM] = v_ref[rows, :]
        vaug_ref[rows, HEAD_DIM:] = jnp.ones((MOBA_BLOCK, HEAD_DIM), BF16)

    k_mean = jnp.concatenate(k_means + [jnp.zeros((nbp - nb, HEAD_DIM), F32)], axis=0).astype(BF16)
    gate_t = lax.dot_general(k_mean, qs_ref[...], (((1,), (1,)), ((), ())),
                             preferred_element_type=F32)

    row_blk = lax.broadcasted_iota(jnp.int32, (nbp, MOBA_BLOCK), 0)
    q_idx = lax.broadcasted_iota(jnp.int32, (MOBA_BLOCK, MOBA_BLOCK), 0)
    k_idx = lax.broadcasted_iota(jnp.int32, (MOBA_BLOCK, MOBA_BLOCK), 1)
    causal = k_idx <= q_idx

    def masked_scores(qb):
        rows = slice(qb * MOBA_BLOCK, (qb + 1) * MOBA_BLOCK)
        nk = (qb + 1) * MOBA_BLOCK
        q_blk = qs_ref[rows, :]
        if qb > MOBA_TOPK:
            g = gate_t[:, rows]
            rank = jnp.zeros((nbp, MOBA_BLOCK), F32)
            for m in range(qb):
                gm = g[m:m + 1, :]
                beats = (gm > g) | ((gm == g) & (m < row_blk))
                rank = rank + beats.astype(F32)
            keep = ((rank < MOBA_TOPK) & (row_blk < qb)) | (row_blk == qb)
            bias_t = jnp.where(keep, 0.0, NEG_INF)
            bias_t = jnp.concatenate([bias_t, jnp.zeros((HEAD_DIM - nbp, MOBA_BLOCK), F32)], axis=0)
            q_aug = jnp.concatenate([q_blk, bias_t.T.astype(BF16)], axis=1)
            s = lax.dot_general(q_aug, kaug_ref[:nk, :], (((1,), (1,)), ((), ())),
                                preferred_element_type=F32)
        else:
            s = lax.dot_general(q_blk, kaug_ref[:nk, :HEAD_DIM], (((1,), (1,)), ((), ())),
                                preferred_element_type=F32)
        s_own = jnp.where(causal, s[:, qb * MOBA_BLOCK:], NEG_INF)
        if qb > 0:
            return jnp.concatenate([s[:, :qb * MOBA_BLOCK], s_own], axis=1)
        return s_own

    def finish(qb, p):
        rows = slice(qb * MOBA_BLOCK, (qb + 1) * MOBA_BLOCK)
        oa = jnp.dot(p, vaug_ref[:(qb + 1) * MOBA_BLOCK, :], preferred_element_type=F32)
        o = oa[:, :HEAD_DIM] / oa[:, HEAD_DIM:]
        o_ref[rows, :] = (o * _silu(g_ref[rows, :].astype(F32))).astype(o_ref.dtype)

    s_cur = masked_scores(0)
    p_prev = None
    yield
    for qb in range(nb):
        s_next = masked_scores(qb + 1) if qb + 1 < nb else None
        if p_prev is not None:
            finish(qb - 1, p_prev)
        m_row = jnp.max(s_cur, axis=-1, keepdims=True)
        p_prev = jnp.exp2(s_cur - m_row).astype(BF16)
        s_cur = s_next
        yield
    finish(nb - 1, p_prev)
    yield


def _rec_steps(layer, q_ref, z_ref, v_ref, g_ref, lbl_ref, nw_ref, o_ref,
               qin_ref, rhs_ref, upd_ref, cum_ref):
    t = q_ref.shape[0]
    n_chunks = t // REC_CHUNK

    logits = lbl_ref[...].astype(F32)
    e = jnp.exp(logits - jnp.max(logits, axis=0, keepdims=True))
    probs = e / jnp.sum(e, axis=0, keepdims=True)
    lb = jnp.sum(probs[:layer + 1], axis=0, keepdims=True)

    def chunk_rows(c):
        return slice(c * REC_CHUNK, (c + 1) * REC_CHUNK)

    mid = REC_CHUNK // 2 - 1
    tri = (lax.broadcasted_iota(jnp.int32, (REC_CHUNK, 2 * REC_CHUNK), 0)
           >= lax.broadcasted_iota(jnp.int32, (REC_CHUNK, 2 * REC_CHUNK), 1) % REC_CHUNK).astype(BF16)
    for c in range(n_chunks):
        rows = chunk_rows(c)
        half_t = 0.5 * jnp.tanh(0.5 * z_ref[rows, :].astype(F32))
        log2_f = jnp.log2(lb + (1.0 - lb) * (0.5 + half_t))
        k_r = (1.0 - lb) * (0.5 - half_t)
        q_r = _silu(q_ref[rows, :].astype(F32)) * (HEAD_DIM ** -0.5)
        high = log2_f.astype(BF16)
        low = (log2_f - high.astype(F32)).astype(BF16)
        cum = jnp.dot(tri, jnp.concatenate([high, low], axis=0), preferred_element_type=F32)
        cum_ref[rows, :] = cum
        rel = cum - cum[mid:mid + 1, :]
        qin_ref[rows, :] = (q_r * jnp.exp2(rel)).astype(BF16)
        rhs_ref[c, HEAD_DIM:, :] = (k_r * jnp.exp2(-rel)).astype(BF16)
        if (c + 1) % REC_GROUP == 0:
            yield

    for c in range(n_chunks):
        upd_ref[c] = lax.dot_general(v_ref[chunk_rows(c), :], rhs_ref[c, HEAD_DIM:, :],
                                     (((0,), (0,)), ((), ())), preferred_element_type=F32)
        if (c + 1) % REC_GROUP == 0:
            yield

    scaled_t = jnp.zeros((HEAD_DIM, HEAD_DIM), F32)
    for c in range(n_chunks):
        rhs_ref[c, :HEAD_DIM, :] = scaled_t.astype(BF16)
        if c + 1 < n_chunks:
            c_mid = cum_ref[c * REC_CHUNK + mid:c * REC_CHUNK + mid + 1, :]
            c_last = cum_ref[(c + 1) * REC_CHUNK - 1:(c + 1) * REC_CHUNK, :]
            next_mid = cum_ref[(c + 1) * REC_CHUNK + mid:(c + 1) * REC_CHUNK + mid + 1, :]
            scaled_t = (scaled_t + upd_ref[c]) * jnp.exp2(c_last - c_mid + next_mid)
        if (c + 1) % REC_GROUP == 0:
            yield

    ri = lax.broadcasted_iota(jnp.int32, (REC_CHUNK, REC_CHUNK), 0)
    ci = lax.broadcasted_iota(jnp.int32, (REC_CHUNK, REC_CHUNK), 1)
    tril = ri >= ci
    nw = nw_ref[...].astype(F32)
    for g0 in range(0, n_chunks, REC_GROUP):
        group = range(g0, min(g0 + REC_GROUP, n_chunks))
        both = [lax.dot_general(qin_ref[chunk_rows(c), :], rhs_ref[c], (((1,), (1,)), ((), ())),
                                preferred_element_type=F32) for c in group]
        for c, b in zip(group, both):
            rows = chunk_rows(c)
            scores = jnp.where(tril, b[:, HEAD_DIM:], 0.0).astype(BF16)
            o = b[:, :HEAD_DIM] + jnp.dot(scores, v_ref[rows, :], preferred_element_type=F32)
            o = o * lax.rsqrt(jnp.mean(o * o, axis=-1, keepdims=True) + NORM_EPS)
            o = o * nw * _silu(g_ref[rows, :].astype(F32))
            o_ref[rows, :] = o.astype(o_ref.dtype)
        yield


def _mixer_kernel(layer, aq, ak, av, ag, cos, sin, rq, rz, rv, rg, lbl, nw, wo, attn_o, rec_o, wo_bf16,
                  kaug, vaug, qs, qin, rhs, upd, cum):
    wo_bf16[...] = wo[...].astype(BF16)
    streams = []
    for i in range(aq.shape[0]):
        lanes = slice(i * HEAD_DIM, (i + 1) * HEAD_DIM)
        streams.append(_attn_steps(aq.at[i], ak.at[i], av.at[i], ag.at[i], cos, sin, attn_o.at[i],
                                   kaug.at[i], vaug.at[i], qs.at[i]))
        streams.append(_rec_steps(layer, rq.at[i], rz.at[i], rv.at[i], rg.at[i], lbl.at[:, lanes],
                                  nw.at[:, lanes], rec_o.at[i], qin.at[i], rhs.at[i], upd.at[i], cum.at[i]))
    stop = object()
    while streams:
        streams = [g for g in streams if next(g, stop) is not stop]


def _mixers(proj, lbl, nw_row, cos_t, sin_t, w_out, *, layer, batch, seq, n_heads):
    n = batch * seq
    mix, d = w_out.shape[1:]
    hp = MIX_HEADS if n_heads % MIX_HEADS == 0 else 1
    n_steps = batch * (n_heads // hp)
    wo_rows = mix // n_steps
    assert wo_rows * n_steps == mix and wo_rows % 16 == 0
    spec = lambda off: pl.BlockSpec((hp, seq, HEAD_DIM), lambda b, h: (off // hp + h, b, 0))
    tbl = pl.BlockSpec((seq, HEAD_DIM), lambda b, h: (0, 0))
    n_chunks = seq // REC_CHUNK
    out = jax.ShapeDtypeStruct((n_heads, n, HEAD_DIM), BF16)
    return pl.pallas_call(
        functools.partial(_mixer_kernel, layer),
        grid=(batch, n_heads // hp),
        in_specs=[
            spec(0), spec(n_heads), spec(2 * n_heads), spec(3 * n_heads), tbl, tbl,
            spec(4 * n_heads), spec(5 * n_heads), spec(6 * n_heads), spec(7 * n_heads),
            pl.BlockSpec((lbl.shape[0], hp * HEAD_DIM), lambda b, h: (0, h)),
            pl.BlockSpec((1, hp * HEAD_DIM), lambda b, h: (0, h)),
            pl.BlockSpec((None, wo_rows, d), lambda b, h: (layer, b * (n_heads // hp) + h, 0)),
        ],
        out_specs=[pl.BlockSpec((hp, seq, HEAD_DIM), lambda b, h: (h, b, 0))] * 2
        + [pl.BlockSpec((wo_rows, d), lambda b, h: (b * (n_heads // hp) + h, 0))],
        out_shape=[out, out, jax.ShapeDtypeStruct((mix, d), BF16)],
        scratch_shapes=[
            pltpu.VMEM((hp, seq, 2 * HEAD_DIM), BF16),
            pltpu.VMEM((hp, seq, 2 * HEAD_DIM), BF16),
            pltpu.VMEM((hp, seq, HEAD_DIM), BF16),
            pltpu.VMEM((hp, seq, HEAD_DIM), BF16),
            pltpu.VMEM((hp, n_chunks, HEAD_DIM + REC_CHUNK, HEAD_DIM), BF16),
            pltpu.VMEM((hp, n_chunks, HEAD_DIM, HEAD_DIM), F32),
            pltpu.VMEM((hp, seq, HEAD_DIM), F32),
        ],
        compiler_params=_params(2),
        name="mixers",
    )(proj, proj, proj, proj, cos_t, sin_t, proj, proj, proj, proj, lbl, nw_row, w_out)


def _out_kernel(final_norm, n_attn_k, n_k, attn_ref, rec_ref, w_ref, x_ref, nw_ref, o_ref):
    kk = pl.program_id(1)
    cw = x_ref.shape[1]

    def partial_product():
        heads = [jnp.where(kk < n_attn_k, attn_ref[c], rec_ref[c]) for c in range(attn_ref.shape[0])]
        return jnp.dot(jnp.concatenate(heads, axis=1), w_ref[...], preferred_element_type=F32)

    def add_residual(j):
        o_ref[:, j * cw:(j + 1) * cw] += x_ref[...]

    def finish():
        if final_norm:
            h = o_ref[...]
            ms = jnp.mean(h * h, axis=-1, keepdims=True)
            o_ref[...] = h * lax.rsqrt(ms + NORM_EPS) * nw_ref[...]

    if n_k == 1:
        o_ref[...] = partial_product()
        add_residual(0)
        finish()
        return

    @pl.when(kk == 0)
    def _():
        o_ref[...] = partial_product()
        add_residual(0)

    @pl.when((kk > 0) & (kk < n_k - 1))
    def _():
        o_ref[...] += partial_product()

    for j in range(1, n_k - 1):
        @pl.when(kk == j)
        def _(j=j):
            add_residual(j)

    @pl.when(kk == n_k - 1)
    def _():
        o_ref[...] += partial_product()
        add_residual(n_k - 1)
        finish()


def _out_proj(attn, rec, w_bf16, x2d, nw_row, *, final_norm, bm=1024, bk=512):
    n, d = x2d.shape
    a_w = attn.shape[0] * HEAD_DIM
    r_w = rec.shape[0] * HEAD_DIM
    bm = _pick_block(n, bm, 8)
    bk = _pick_block(int(np.gcd(a_w, r_w)), bk, 128)
    n_attn_k = a_w // bk
    n_k = (a_w + r_w) // bk
    assert d % (n_k * 128) == 0
    return pl.pallas_call(
        functools.partial(_out_kernel, final_norm, n_attn_k, n_k),
        grid=(n // bm, n_k),
        in_specs=[
            pl.BlockSpec((bk // HEAD_DIM, bm, HEAD_DIM), lambda i, k: (jnp.minimum(k, n_attn_k - 1), i, 0)),
            pl.BlockSpec((bk // HEAD_DIM, bm, HEAD_DIM), lambda i, k: (jnp.maximum(k - n_attn_k, 0), i, 0)),
            pl.BlockSpec((bk, d), lambda i, k: (k, 0)),
            pl.BlockSpec((bm, d // n_k), lambda i, k: (i, k)),
            pl.BlockSpec((1, d), lambda i, k: (0, 0)),
        ],
        out_specs=pl.BlockSpec((bm, d), lambda i, k: (i, 0)),
        out_shape=jax.ShapeDtypeStruct((n, d), F32),
        compiler_params=_params(2),
        name="out_proj",
    )(attn, rec, w_bf16, x2d, nw_row)


def _rope_tables(seq):
    half = ROPE_DIM // 2
    inv_freq = jnp.power(ROPE_THETA, -jnp.arange(half, dtype=F32) / half)
    ang = jnp.arange(seq, dtype=F32)[:, None] * inv_freq[None, :]
    cos, sin = jnp.cos(ang), jnp.sin(ang)
    rest = HEAD_DIM - ROPE_DIM
    cos_t = jnp.concatenate([cos, cos, jnp.ones((seq, rest), F32)], axis=1)
    sin_t = jnp.concatenate([-sin, sin, jnp.zeros((seq, rest), F32)], axis=1)
    return cos_t, sin_t


def kernel(x, norm_w, w_in, rec_lower_bound_logits, rec_out_norm_w, w_out, final_norm_w):
    batch, seq, d_model = x.shape
    depth = norm_w.shape[0]
    mix = w_out.shape[1]
    attn_w = mix // 2
    rec_w = mix - attn_w
    n_ah = attn_w // HEAD_DIM
    n_rh = rec_w // HEAD_DIM
    key_w = n_rh * HEAD_DIM
    assert seq % MOBA_BLOCK == 0 and seq % REC_CHUNK == 0
    assert n_ah == n_rh
    assert w_in.shape[2] == 4 * attn_w + 2 * key_w + 2 * rec_w

    cos_t, sin_t = _rope_tables(seq)
    h = x.reshape(batch * seq, d_model)
    for layer in range(depth):
        proj = _norm_proj(h, norm_w[layer].reshape(1, d_model), w_in, layer)
        rec_nw = rec_out_norm_w[layer].reshape(1, rec_w)
        attn, rec, w_out_bf16 = _mixers(proj, rec_lower_bound_logits, rec_nw, cos_t, sin_t, w_out,
                                        layer=layer, batch=batch, seq=seq, n_heads=n_ah)
        last = layer == depth - 1
        h = _out_proj(attn, rec, w_out_bf16, h, final_norm_w.reshape(1, d_model),
                      final_norm=last)
    return h.reshape(batch, seq, d_model)
```

```python
import functools

import jax
import jax.numpy as jnp
import numpy as np
from jax import lax
from jax.experimental import pallas as pl
from jax.experimental.pallas import tpu as pltpu

HEAD_DIM = 128
ROPE_THETA = 500000.0
ROPE_DIM = HEAD_DIM // 4
MOBA_BLOCK = 256
MOBA_TOPK = 3
REC_CHUNK = 64
REC_GROUP = 8
MIX_HEADS = 2
NORM_EPS = 1e-6
NEG_INF = -1e30

VMEM_LIMIT_BYTES = 60 * 1024 * 1024

F32 = jnp.float32
BF16 = jnp.bfloat16


def _params(n_grid_dims):
    return pltpu.CompilerParams(
        dimension_semantics=("arbitrary",) * n_grid_dims,
        vmem_limit_bytes=VMEM_LIMIT_BYTES,
    )


def _pick_block(dim, target, align):
    best = None
    for b in range(align, min(dim, target) + 1, align):
        if dim % b == 0:
            best = b
    assert best is not None, (dim, target, align)
    return best


def _sigmoid(x):
    return 0.5 * jnp.tanh(0.5 * x) + 0.5


def _silu(x):
    return x * _sigmoid(x)


def _norm_proj_kernel(x_ref, nw_ref, w_ref, o_ref, u_even, u_odd):
    i = pl.program_id(0)
    j = pl.program_id(1)
    strip = x_ref.shape[0]

    def normalize_into(u_dst):
        x = x_ref[...]
        ms = jnp.mean(x * x, axis=-1, keepdims=True)
        rows = pl.ds(pl.multiple_of(j * strip, strip), strip)
        u_dst[rows, :] = (x * lax.rsqrt(ms + NORM_EPS) * nw_ref[...]).astype(BF16)

    def project(u_src):
        acc = jnp.dot(u_src[...], w_ref[...].astype(BF16), preferred_element_type=F32)
        for c in range(o_ref.shape[0]):
            o_ref[c] = acc[:, c * HEAD_DIM:(c + 1) * HEAD_DIM].astype(o_ref.dtype)

    @pl.when(i == 0)
    def _():
        normalize_into(u_even)

    @pl.when((i > 0) & (i % 2 == 1))
    def _():
        project(u_even)
        normalize_into(u_odd)

    @pl.when((i > 0) & (i % 2 == 0))
    def _():
        project(u_odd)
        normalize_into(u_even)


def _norm_proj(x2d, nw_row, w, layer, *, bm=2048, bn=512):
    n, d = x2d.shape
    in_w = w.shape[2]
    bm = _pick_block(n, bm, 8)
    bn = _pick_block(in_w, bn, 128)
    n_i, n_j = n // bm, in_w // bn
    strip = bm // n_j
    assert strip * n_j == bm and strip % 8 == 0, (bm, n_j)
    last_strip = n // strip - 1
    return pl.pallas_call(
        _norm_proj_kernel,
        grid=(n_i + 1, n_j),
        in_specs=[
            pl.BlockSpec((strip, d), lambda i, j: (jnp.minimum(i * n_j + j, last_strip), 0)),
            pl.BlockSpec((1, d), lambda i, j: (0, 0)),
            pl.BlockSpec((None, d, bn), lambda i, j: (layer, 0, jnp.where(i == 0, 0, j))),
        ],
        out_specs=pl.BlockSpec((bn // HEAD_DIM, bm, HEAD_DIM),
                               lambda i, j: (jnp.where(i == 0, 0, j), jnp.maximum(i - 1, 0), 0)),
        out_shape=jax.ShapeDtypeStruct((in_w // HEAD_DIM, n, HEAD_DIM), BF16),
        scratch_shapes=[pltpu.VMEM((bm, d), BF16), pltpu.VMEM((bm, d), BF16)],
        compiler_params=_params(2),
        name="norm_in_proj",
    )(x2d, nw_row, w)


def _attn_steps(q_ref, k_ref, v_ref, g_ref, cos_ref, sin_ref, o_ref, kaug_ref, vaug_ref, qs_ref):
    t = q_ref.shape[0]
    nb = t // MOBA_BLOCK
    nbp = max(16, -(-nb // 16) * 16)
    half = ROPE_DIM // 2

    src = lax.broadcasted_iota(jnp.int32, (HEAD_DIM, HEAD_DIM), 0)
    dst = lax.broadcasted_iota(jnp.int32, (HEAD_DIM, HEAD_DIM), 1)
    pick = ((dst < half) & (src == dst + half)) | ((dst >= half) & (dst < 2 * half) & (src == dst - half))
    pick = pick.astype(BF16)
    lane = lax.broadcasted_iota(jnp.int32, (MOBA_BLOCK, HEAD_DIM), 1)

    k_means = []
    for blk in range(nb):
        rows = slice(blk * MOBA_BLOCK, (blk + 1) * MOBA_BLOCK)
        cos = cos_ref[rows, :]
        sin = sin_ref[rows, :]

        def rope(x_bf16):
            partner = jnp.dot(x_bf16, pick, preferred_element_type=F32)
            return x_bf16.astype(F32) * cos + partner * sin

        k = rope(k_ref[rows, :])
        kaug_ref[rows, :HEAD_DIM] = k.astype(BF16)
        kaug_ref[rows, HEAD_DIM:] = (lane == blk).astype(BF16)
        k_means.append(jnp.mean(k, axis=0, keepdims=True))
        qs_ref[rows, :] = (rope(q_ref[rows, :]) * float(HEAD_DIM ** -0.5 * np.log2(np.e))).astype(BF16)
        vaug_ref[rows, :HEAD_DIM] = v_ref[rows, :]
        vaug_ref[rows, HEAD_DIM:] = jnp.ones((MOBA_BLOCK, HEAD_DIM), BF16)

    k_mean = jnp.concatenate(k_means + [jnp.zeros((nbp - nb, HEAD_DIM), F32)], axis=0).astype(BF16)
    gate_t = lax.dot_general(k_mean, qs_ref[...], (((1,), (1,)), ((), ())),
                             preferred_element_type=F32)

    row_blk = lax.broadcasted_iota(jnp.int32, (nbp, MOBA_BLOCK), 0)
    q_idx = lax.broadcasted_iota(jnp.int32, (MOBA_BLOCK, MOBA_BLOCK), 0)
    k_idx = lax.broadcasted_iota(jnp.int32, (MOBA_BLOCK, MOBA_BLOCK), 1)
    causal = k_idx <= q_idx

    def masked_scores(qb):
        rows = slice(qb * MOBA_BLOCK, (qb + 1) * MOBA_BLOCK)
        nk = (qb + 1) * MOBA_BLOCK
        q_blk = qs_ref[rows, :]
        if qb > MOBA_TOPK:
            g = gate_t[:, rows]
            rank = jnp.zeros((nbp, MOBA_BLOCK), F32)
            for m in range(qb):
                gm = g[m:m + 1, :]
                beats = (gm > g) | ((gm == g) & (m < row_blk))
                rank = rank + beats.astype(F32)
            keep = ((rank < MOBA_TOPK) & (row_blk < qb)) | (row_blk == qb)
            bias_t = jnp.where(keep, 0.0, NEG_INF)
            bias_t = jnp.concatenate([bias_t, jnp.zeros((HEAD_DIM - nbp, MOBA_BLOCK), F32)], axis=0)
            q_aug = jnp.concatenate([q_blk, bias_t.T.astype(BF16)], axis=1)
            s = lax.dot_general(q_aug, kaug_ref[:nk, :], (((1,), (1,)), ((), ())),
                                preferred_element_type=F32)
        else:
            s = lax.dot_general(q_blk, kaug_ref[:nk, :HEAD_DIM], (((1,), (1,)), ((), ())),
                                preferred_element_type=F32)
        s_own = jnp.where(causal, s[:, qb * MOBA_BLOCK:], NEG_INF)
        if qb > 0:
            return jnp.concatenate([s[:, :qb * MOBA_BLOCK], s_own], axis=1)
        return s_own

    def finish(qb, p):
        rows = slice(qb * MOBA_BLOCK, (qb + 1) * MOBA_BLOCK)
        oa = jnp.dot(p, vaug_ref[:(qb + 1) * MOBA_BLOCK, :], preferred_element_type=F32)
        o = oa[:, :HEAD_DIM] / oa[:, HEAD_DIM:]
        o_ref[rows, :] = (o * _silu(g_ref[rows, :].astype(F32))).astype(o_ref.dtype)

    s_cur = masked_scores(0)
    p_prev = None
    yield
    for qb in range(nb):
        s_next = masked_scores(qb + 1) if qb + 1 < nb else None
        if p_prev is not None:
            finish(qb - 1, p_prev)
        m_row = jnp.max(s_cur, axis=-1, keepdims=True)
        p_prev = jnp.exp2(s_cur - m_row).astype(BF16)
        s_cur = s_next
        yield
    finish(nb - 1, p_prev)
    yield


def _rec_steps(layer, q_ref, z_ref, v_ref, g_ref, lbl_ref, nw_ref, o_ref,
               qin_ref, rhs_ref, upd_ref, cum_ref):
    t = q_ref.shape[0]
    n_chunks = t // REC_CHUNK

    logits = lbl_ref[...].astype(F32)
    e = jnp.exp(logits - jnp.max(logits, axis=0, keepdims=True))
    probs = e / jnp.sum(e, axis=0, keepdims=True)
    lb = jnp.sum(probs[:layer + 1], axis=0, keepdims=True)

    def chunk_rows(c):
        return slice(c * REC_CHUNK, (c + 1) * REC_CHUNK)

    mid = REC_CHUNK // 2 - 1
    tri = (lax.broadcasted_iota(jnp.int32, (REC_CHUNK, 2 * REC_CHUNK), 0)
           >= lax.broadcasted_iota(jnp.int32, (REC_CHUNK, 2 * REC_CHUNK), 1) % REC_CHUNK).astype(BF16)
    for c in range(n_chunks):
        rows = chunk_rows(c)
        half_t = 0.5 * jnp.tanh(0.5 * z_ref[rows, :].astype(F32))
        log2_f = jnp.log2(lb + (1.0 - lb) * (0.5 + half_t))
        k_r = (1.0 - lb) * (0.5 - half_t)
        q_r = _silu(q_ref[rows, :].astype(F32)) * (HEAD_DIM ** -0.5)
        high = log2_f.astype(BF16)
        low = (log2_f - high.astype(F32)).astype(BF16)
        cum = jnp.dot(tri, jnp.concatenate([high, low], axis=0), preferred_element_type=F32)
        cum_ref[rows, :] = cum
        rel = cum - cum[mid:mid + 1, :]
        qin_ref[rows, :] = (q_r * jnp.exp2(rel)).astype(BF16)
        rhs_ref[c, HEAD_DIM:, :] = (k_r * jnp.exp2(-rel)).astype(BF16)
        if (c + 1) % REC_GROUP == 0:
            yield

    for c in range(n_chunks):
        upd_ref[c] = lax.dot_general(v_ref[chunk_rows(c), :], rhs_ref[c, HEAD_DIM:, :],
                                     (((0,), (0,)), ((), ())), preferred_element_type=F32)
        if (c + 1) % REC_GROUP == 0:
            yield

    scaled_t = jnp.zeros((HEAD_DIM, HEAD_DIM), F32)
    for c in range(n_chunks):
        rhs_ref[c, :HEAD_DIM, :] = scaled_t.astype(BF16)
        if c + 1 < n_chunks:
            c_mid = cum_ref[c * REC_CHUNK + mid:c * REC_CHUNK + mid + 1, :]
            c_last = cum_ref[(c + 1) * REC_CHUNK - 1:(c + 1) * REC_CHUNK, :]
            next_mid = cum_ref[(c + 1) * REC_CHUNK + mid:(c + 1) * REC_CHUNK + mid + 1, :]
            scaled_t = (scaled_t + upd_ref[c]) * jnp.exp2(c_last - c_mid + next_mid)
        if (c + 1) % REC_GROUP == 0:
            yield

    ri = lax.broadcasted_iota(jnp.int32, (REC_CHUNK, REC_CHUNK), 0)
    ci = lax.broadcasted_iota(jnp.int32, (REC_CHUNK, REC_CHUNK), 1)
    tril = ri >= ci
    nw = nw_ref[...].astype(F32)
    for g0 in range(0, n_chunks, REC_GROUP):
        group = range(g0, min(g0 + REC_GROUP, n_chunks))
        both = [lax.dot_general(qin_ref[chunk_rows(c), :], rhs_ref[c], (((1,), (1,)), ((), ())),
                                preferred_element_type=F32) for c in group]
        for c, b in zip(group, both):
            rows = chunk_rows(c)
            scores = jnp.where(tril, b[:, HEAD_DIM:], 0.0).astype(BF16)
            o = b[:, :HEAD_DIM] + jnp.dot(scores, v_ref[rows, :], preferred_element_type=F32)
            o = o * lax.rsqrt(jnp.mean(o * o, axis=-1, keepdims=True) + NORM_EPS)
            o = o * nw * _silu(g_ref[rows, :].astype(F32))
            o_ref[rows, :] = o.astype(o_ref.dtype)
        yield


def _mixer_kernel(layer, aq, ak, av, ag, cos, sin, rq, rz, rv, rg, lbl, nw, wo, attn_o, rec_o, wo_bf16,
                  kaug, vaug, qs, qin, rhs, upd, cum):
    wo_bf16[...] = wo[...].astype(BF16)
    streams = []
    for i in range(aq.shape[0]):
        lanes = slice(i * HEAD_DIM, (i + 1) * HEAD_DIM)
        streams.append(_attn_steps(aq.at[i], ak.at[i], av.at[i], ag.at[i], cos, sin, attn_o.at[i],
                                   kaug.at[i], vaug.at[i], qs.at[i]))
        streams.append(_rec_steps(layer, rq.at[i], rz.at[i], rv.at[i], rg.at[i], lbl.at[:, lanes],
                                  nw.at[:, lanes], rec_o.at[i], qin.at[i], rhs.at[i], upd.at[i], cum.at[i]))
    stop = object()
    while streams:
        streams = [g for g in streams if next(g, stop) is not stop]


def _mixers(proj, lbl, nw_row, cos_t, sin_t, w_out, *, layer, batch, seq, n_heads):
    n = batch * seq
    mix, d = w_out.shape[1:]
    hp = MIX_HEADS if n_heads % MIX_HEADS == 0 else 1
    n_steps = batch * (n_heads // hp)
    wo_rows = mix // n_steps
    assert wo_rows * n_steps == mix and wo_rows % 16 == 0
    spec = lambda off: pl.BlockSpec((hp, seq, HEAD_DIM), lambda b, h: (off // hp + h, b, 0))
    tbl = pl.BlockSpec((seq, HEAD_DIM), lambda b, h: (0, 0))
    n_chunks = seq // REC_CHUNK
    out = jax.ShapeDtypeStruct((n_heads, n, HEAD_DIM), BF16)
    return pl.pallas_call(
        functools.partial(_mixer_kernel, layer),
        grid=(batch, n_heads // hp),
        in_specs=[
            spec(0), spec(n_heads), spec(2 * n_heads), spec(3 * n_heads), tbl, tbl,
            spec(4 * n_heads), spec(5 * n_heads), spec(6 * n_heads), spec(7 * n_heads),
            pl.BlockSpec((lbl.shape[0], hp * HEAD_DIM), lambda b, h: (0, h)),
            pl.BlockSpec((1, hp * HEAD_DIM), lambda b, h: (0, h)),
            pl.BlockSpec((None, wo_rows, d), lambda b, h: (layer, b * (n_heads // hp) + h, 0)),
        ],
        out_specs=[pl.BlockSpec((hp, seq, HEAD_DIM), lambda b, h: (h, b, 0))] * 2
        + [pl.BlockSpec((wo_rows, d), lambda b, h: (b * (n_heads // hp) + h, 0))],
        out_shape=[out, out, jax.ShapeDtypeStruct((mix, d), BF16)],
        scratch_shapes=[
            pltpu.VMEM((hp, seq, 2 * HEAD_DIM), BF16),
            pltpu.VMEM((hp, seq, 2 * HEAD_DIM), BF16),
            pltpu.VMEM((hp, seq, HEAD_DIM), BF16),
            pltpu.VMEM((hp, seq, HEAD_DIM), BF16),
            pltpu.VMEM((hp, n_chunks, HEAD_DIM + REC_CHUNK, HEAD_DIM), BF16),
            pltpu.VMEM((hp, n_chunks, HEAD_DIM, HEAD_DIM), F32),
            pltpu.VMEM((hp, seq, HEAD_DIM), F32),
        ],
        compiler_params=_params(2),
        name="mixers",
    )(proj, proj, proj, proj, cos_t, sin_t, proj, proj, proj, proj, lbl, nw_row, w_out)


def _out_kernel(final_norm, n_attn_k, n_k, attn_ref, rec_ref, w_ref, x_ref, nw_ref, o_ref, acc_ref):
    kk = pl.program_id(1)
    rs, cw = x_ref.shape
    rows = pl.ds(pl.multiple_of(pl.program_id(2) * rs, rs), rs)

    def partial_product():
        heads = [jnp.where(kk < n_attn_k, attn_ref[c], rec_ref[c]) for c in range(attn_ref.shape[0])]
        return jnp.dot(jnp.concatenate(heads, axis=1), w_ref[...], preferred_element_type=F32)

    def add_residual(j):
        acc_ref[rows, j * cw:(j + 1) * cw] += x_ref[...]

    def finish():
        h = acc_ref[rows, :]
        if final_norm:
            ms = jnp.mean(h * h, axis=-1, keepdims=True)
            h = h * lax.rsqrt(ms + NORM_EPS) * nw_ref[...]
        o_ref[...] = h

    if n_k == 1:
        acc_ref[rows, :] = partial_product()
        add_residual(0)
        finish()
        return

    @pl.when(kk == 0)
    def _():
        acc_ref[rows, :] = partial_product()
        add_residual(0)

    @pl.when((kk > 0) & (kk < n_k - 1))
    def _():
        acc_ref[rows, :] += partial_product()

    for j in range(1, n_k - 1):
        @pl.when(kk == j)
        def _(j=j):
            add_residual(j)

    @pl.when(kk == n_k - 1)
    def _():
        acc_ref[rows, :] += partial_product()
        add_residual(n_k - 1)
        finish()


def _out_proj(attn, rec, w_bf16, x2d, nw_row, *, final_norm, bm=1024, rs=512, bk=1024):
    n, d = x2d.shape
    a_w = attn.shape[0] * HEAD_DIM
    r_w = rec.shape[0] * HEAD_DIM
    bm = _pick_block(n, bm, 8)
    rs = _pick_block(bm, rs, 8)
    n_r = bm // rs
    bk = _pick_block(int(np.gcd(a_w, r_w)), bk, 128)
    n_attn_k = a_w // bk
    n_k = (a_w + r_w) // bk
    assert d % (n_k * 128) == 0

    def group(i, r):
        return i * n_r + r

    return pl.pallas_call(
        functools.partial(_out_kernel, final_norm, n_attn_k, n_k),
        grid=(n // bm, n_k, n_r),
        in_specs=[
            pl.BlockSpec((bk // HEAD_DIM, rs, HEAD_DIM),
                         lambda i, k, r: (jnp.minimum(k, n_attn_k - 1), group(i, r), 0)),
            pl.BlockSpec((bk // HEAD_DIM, rs, HEAD_DIM),
                         lambda i, k, r: (jnp.maximum(k - n_attn_k, 0), group(i, r), 0)),
            pl.BlockSpec((bk, d), lambda i, k, r: (k, 0)),
            pl.BlockSpec((rs, d // n_k), lambda i, k, r: (group(i, r), k)),
            pl.BlockSpec((1, d), lambda i, k, r: (0, 0)),
        ],
        out_specs=pl.BlockSpec((rs, d), lambda i, k, r: (group(i, jnp.where(k == n_k - 1, r, 0)), 0)),
        out_shape=jax.ShapeDtypeStruct((n, d), F32),
        scratch_shapes=[pltpu.VMEM((bm, d), F32)],
        compiler_params=_params(3),
        name="out_proj",
    )(attn, rec, w_bf16, x2d, nw_row)


def _rope_tables(seq):
    half = ROPE_DIM // 2
    inv_freq = jnp.power(ROPE_THETA, -jnp.arange(half, dtype=F32) / half)
    ang = jnp.arange(seq, dtype=F32)[:, None] * inv_freq[None, :]
    cos, sin = jnp.cos(ang), jnp.sin(ang)
    rest = HEAD_DIM - ROPE_DIM
    cos_t = jnp.concatenate([cos, cos, jnp.ones((seq, rest), F32)], axis=1)
    sin_t = jnp.concatenate([-sin, sin, jnp.zeros((seq, rest), F32)], axis=1)
    return cos_t, sin_t


def kernel(x, norm_w, w_in, rec_lower_bound_logits, rec_out_norm_w, w_out, final_norm_w):
    batch, seq, d_model = x.shape
    depth = norm_w.shape[0]
    mix = w_out.shape[1]
    attn_w = mix // 2
    rec_w = mix - attn_w
    n_ah = attn_w // HEAD_DIM
    n_rh = rec_w // HEAD_DIM
    key_w = n_rh * HEAD_DIM
    assert seq % MOBA_BLOCK == 0 and seq % REC_CHUNK == 0
    assert n_ah == n_rh
    assert w_in.shape[2] == 4 * attn_w + 2 * key_w + 2 * rec_w

    cos_t, sin_t = _rope_tables(seq)
    h = x.reshape(batch * seq, d_model)
    for layer in range(depth):
        proj = _norm_proj(h, norm_w[layer].reshape(1, d_model), w_in, layer)
        rec_nw = rec_out_norm_w[layer].reshape(1, rec_w)
        attn, rec, w_out_bf16 = _mixers(proj, rec_lower_bound_logits, rec_nw, cos_t, sin_t, w_out,
                                        layer=layer, batch=batch, seq=seq, n_heads=n_ah)
        last = layer == depth - 1
        h = _out_proj(attn, rec, w_out_bf16, h, final_norm_w.reshape(1, d_model),
                      final_norm=last)
    return h.reshape(batch, seq, d_model)
```

```python
import functools

import jax
import jax.numpy as jnp
import numpy as np
from jax import lax
from jax.experimental import pallas as pl
from jax.experimental.pallas import tpu as pltpu

HEAD_DIM = 128
ROPE_THETA = 500000.0
ROPE_DIM = HEAD_DIM // 4
MOBA_BLOCK = 256
MOBA_TOPK = 3
REC_CHUNK = 64
REC_GROUP = 8
MIX_HEADS = 2
N_KINDS = 8
NORM_EPS = 1e-6
NEG_INF = -1e30

VMEM_LIMIT_BYTES = 60 * 1024 * 1024

F32 = jnp.float32
BF16 = jnp.bfloat16


def _params(n_grid_dims):
    return pltpu.CompilerParams(
        dimension_semantics=("arbitrary",) * n_grid_dims,
        vmem_limit_bytes=VMEM_LIMIT_BYTES,
    )


def _pick_block(dim, target, align):
    best = None
    for b in range(align, min(dim, target) + 1, align):
        if dim % b == 0:
            best = b
    assert best is not None, (dim, target, align)
    return best


def _sigmoid(x):
    return 0.5 * jnp.tanh(0.5 * x) + 0.5


def _silu(x):
    return x * _sigmoid(x)


def _norm_proj_kernel(x_ref, nw_ref, w_ref, o_ref, u_even, u_odd):
    i = pl.program_id(0)
    j = pl.program_id(1)
    strip = x_ref.shape[0]

    def normalize_into(u_dst):
        x = x_ref[...]
        ms = jnp.mean(x * x, axis=-1, keepdims=True)
        rows = pl.ds(pl.multiple_of(j * strip, strip), strip)
        u_dst[rows, :] = (x * lax.rsqrt(ms + NORM_EPS) * nw_ref[...]).astype(BF16)

    def project(u_src):
        acc = jnp.dot(u_src[...], w_ref[...].astype(BF16), preferred_element_type=F32)
        for c in range(o_ref.shape[0]):
            o_ref[c] = acc[:, c * HEAD_DIM:(c + 1) * HEAD_DIM].astype(o_ref.dtype)

    @pl.when(i == 0)
    def _():
        normalize_into(u_even)

    @pl.when((i > 0) & (i % 2 == 1))
    def _():
        project(u_even)
        normalize_into(u_odd)

    @pl.when((i > 0) & (i % 2 == 0))
    def _():
        project(u_odd)
        normalize_into(u_even)


def _norm_proj(x2d, nw_row, w, layer, *, bm=2048, bn=512):
    n, d = x2d.shape
    in_w = w.shape[2]
    kind_w = in_w // N_KINDS
    bm = _pick_block(n, bm, 8)
    bn = _pick_block(kind_w, bn, 128)
    n_groups = kind_w // bn
    n_i, n_j = n // bm, in_w // bn
    strip = bm // n_j
    assert strip * n_j == bm and strip % 8 == 0, (bm, n_j)
    last_strip = n // strip - 1

    def w_tile(j):
        return (j % N_KINDS) * n_groups + j // N_KINDS

    return pl.pallas_call(
        _norm_proj_kernel,
        grid=(n_i + 1, n_j),
        in_specs=[
            pl.BlockSpec((strip, d), lambda i, j: (jnp.minimum(i * n_j + j, last_strip), 0)),
            pl.BlockSpec((1, d), lambda i, j: (0, 0)),
            pl.BlockSpec((None, d, bn), lambda i, j: (layer, 0, w_tile(jnp.where(i == 0, 0, j)))),
        ],
        out_specs=pl.BlockSpec(
            (None, None, bn // HEAD_DIM, bm, HEAD_DIM),
            lambda i, j: (jnp.where(i == 0, 0, j // N_KINDS), jnp.where(i == 0, 0, j % N_KINDS),
                          0, jnp.maximum(i - 1, 0), 0)),
        out_shape=jax.ShapeDtypeStruct((n_groups, N_KINDS, bn // HEAD_DIM, n, HEAD_DIM), BF16),
        scratch_shapes=[pltpu.VMEM((bm, d), BF16), pltpu.VMEM((bm, d), BF16)],
        compiler_params=_params(2),
        name="norm_in_proj",
    )(x2d, nw_row, w)


def _attn_steps(q_ref, k_ref, v_ref, g_ref, cos_ref, sin_ref, o_ref, kaug_ref, vaug_ref, qs_ref):
    t = q_ref.shape[0]
    nb = t // MOBA_BLOCK
    nbp = max(16, -(-nb // 16) * 16)
    half = ROPE_DIM // 2

    src = lax.broadcasted_iota(jnp.int32, (HEAD_DIM, HEAD_DIM), 0)
    dst = lax.broadcasted_iota(jnp.int32, (HEAD_DIM, HEAD_DIM), 1)
    pick = ((dst < half) & (src == dst + half)) | ((dst >= half) & (dst < 2 * half) & (src == dst - half))
    pick = pick.astype(BF16)
    lane = lax.broadcasted_iota(jnp.int32, (MOBA_BLOCK, HEAD_DIM), 1)

    k_means = []
    for blk in range(nb):
        rows = slice(blk * MOBA_BLOCK, (blk + 1) * MOBA_BLOCK)
        cos = cos_ref[rows, :]
        sin = sin_ref[rows, :]

        def rope(x_bf16):
            partner = jnp.dot(x_bf16, pick, preferred_element_type=F32)
            return x_bf16.astype(F32) * cos + partner * sin

        k = rope(k_ref[rows, :])
        kaug_ref[rows, :HEAD_DIM] = k.astype(BF16)
        kaug_ref[rows, HEAD_DIM:] = (lane == blk).astype(BF16)
        k_means.append(jnp.mean(k, axis=0, keepdims=True))
        qs_ref[rows, :] = (rope(q_ref[rows, :]) * float(HEAD_DIM ** -0.5 * np.log2(np.e))).astype(BF16)
        vaug_ref[rows, :HEAD_DIM] = v_ref[rows, :]
        vaug_ref[rows, HEAD_DIM:] = jnp.ones((MOBA_BLOCK, HEAD_DIM), BF16)

    k_mean = jnp.concatenate(k_means + [jnp.zeros((nbp - nb, HEAD_DIM), F32)], axis=0).astype(BF16)
    gate_t = lax.dot_general(k_mean, qs_ref[...], (((1,), (1,)), ((), ())),
                             preferred_element_type=F32)

    row_blk = lax.broadcasted_iota(jnp.int32, (nbp, MOBA_BLOCK), 0)
    q_idx = lax.broadcasted_iota(jnp.int32, (MOBA_BLOCK, MOBA_BLOCK), 0)
    k_idx = lax.broadcasted_iota(jnp.int32, (MOBA_BLOCK, MOBA_BLOCK), 1)
    causal = k_idx <= q_idx

    def masked_scores(qb):
        rows = slice(qb * MOBA_BLOCK, (qb + 1) * MOBA_BLOCK)
        nk = (qb + 1) * MOBA_BLOCK
        q_blk = qs_ref[rows, :]
        if qb > MOBA_TOPK:
            g = gate_t[:, rows]
            rank = jnp.zeros((nbp, MOBA_BLOCK), F32)
            for m in range(qb):
                gm = g[m:m + 1, :]
                beats = (gm > g) | ((gm == g) & (m < row_blk))
                rank = rank + beats.astype(F32)
            keep = ((rank < MOBA_TOPK) & (row_blk < qb)) | (row_blk == qb)
            bias_t = jnp.where(keep, 0.0, NEG_INF)
            bias_t = jnp.concatenate([bias_t, jnp.zeros((HEAD_DIM - nbp, MOBA_BLOCK), F32)], axis=0)
            q_aug = jnp.concatenate([q_blk, bias_t.T.astype(BF16)], axis=1)
            s = lax.dot_general(q_aug, kaug_ref[:nk, :], (((1,), (1,)), ((), ())),
                                preferred_element_type=F32)
        else:
            s = lax.dot_general(q_blk, kaug_ref[:nk, :HEAD_DIM], (((1,), (1,)), ((), ())),
                                preferred_element_type=F32)
        s_own = jnp.where(causal, s[:, qb * MOBA_BLOCK:], NEG_INF)
        if qb > 0:
            return jnp.concatenate([s[:, :qb * MOBA_BLOCK], s_own], axis=1)
        return s_own

    def finish(qb, p):
        rows = slice(qb * MOBA_BLOCK, (qb + 1) * MOBA_BLOCK)
        oa = jnp.dot(p, vaug_ref[:(qb + 1) * MOBA_BLOCK, :], preferred_element_type=F32)
        o = oa[:, :HEAD_DIM] / oa[:, HEAD_DIM:]
        o_ref[rows, :] = (o * _silu(g_ref[rows, :].astype(F32))).astype(o_ref.dtype)

    s_cur = masked_scores(0)
    p_prev = None
    yield
    for qb in range(nb):
        s_next = masked_scores(qb + 1) if qb + 1 < nb else None
        if p_prev is not None:
            finish(qb - 1, p_prev)
        m_row = jnp.max(s_cur, axis=-1, keepdims=True)
        p_prev = jnp.exp2(s_cur - m_row).astype(BF16)
        s_cur = s_next
        yield
    finish(nb - 1, p_prev)
    yield


def _rec_steps(layer, q_ref, z_ref, v_ref, g_ref, lbl_ref, nw_ref, o_ref,
               qin_ref, rhs_ref, upd_ref, cum_ref):
    t = q_ref.shape[0]
    n_chunks = t // REC_CHUNK

    logits = lbl_ref[...].astype(F32)
    e = jnp.exp(logits - jnp.max(logits, axis=0, keepdims=True))
    probs = e / jnp.sum(e, axis=0, keepdims=True)
    lb = jnp.sum(probs[:layer + 1], axis=0, keepdims=True)

    def chunk_rows(c):
        return slice(c * REC_CHUNK, (c + 1) * REC_CHUNK)

    mid = REC_CHUNK // 2 - 1
    tri = (lax.broadcasted_iota(jnp.int32, (REC_CHUNK, 2 * REC_CHUNK), 0)
           >= lax.broadcasted_iota(jnp.int32, (REC_CHUNK, 2 * REC_CHUNK), 1) % REC_CHUNK).astype(BF16)
    for c in range(n_chunks):
        rows = chunk_rows(c)
        half_t = 0.5 * jnp.tanh(0.5 * z_ref[rows, :].astype(F32))
        log2_f = jnp.log2(lb + (1.0 - lb) * (0.5 + half_t))
        k_r = (1.0 - lb) * (0.5 - half_t)
        q_r = _silu(q_ref[rows, :].astype(F32)) * (HEAD_DIM ** -0.5)
        high = log2_f.astype(BF16)
        low = (log2_f - high.astype(F32)).astype(BF16)
        cum = jnp.dot(tri, jnp.concatenate([high, low], axis=0), preferred_element_type=F32)
        cum_ref[rows, :] = cum
        rel = cum - cum[mid:mid + 1, :]
        qin_ref[rows, :] = (q_r * jnp.exp2(rel)).astype(BF16)
        rhs_ref[c, HEAD_DIM:, :] = (k_r * jnp.exp2(-rel)).astype(BF16)
        if (c + 1) % REC_GROUP == 0:
            yield

    for c in range(n_chunks):
        upd_ref[c] = lax.dot_general(v_ref[chunk_rows(c), :], rhs_ref[c, HEAD_DIM:, :],
                                     (((0,), (0,)), ((), ())), preferred_element_type=F32)
        if (c + 1) % REC_GROUP == 0:
            yield

    scaled_t = jnp.zeros((HEAD_DIM, HEAD_DIM), F32)
    for c in range(n_chunks):
        rhs_ref[c, :HEAD_DIM, :] = scaled_t.astype(BF16)
        if c + 1 < n_chunks:
            c_mid = cum_ref[c * REC_CHUNK + mid:c * REC_CHUNK + mid + 1, :]
            c_last = cum_ref[(c + 1) * REC_CHUNK - 1:(c + 1) * REC_CHUNK, :]
            next_mid = cum_ref[(c + 1) * REC_CHUNK + mid:(c + 1) * REC_CHUNK + mid + 1, :]
            scaled_t = (scaled_t + upd_ref[c]) * jnp.exp2(c_last - c_mid + next_mid)
        if (c + 1) % REC_GROUP == 0:
            yield

    ri = lax.broadcasted_iota(jnp.int32, (REC_CHUNK, REC_CHUNK), 0)
    ci = lax.broadcasted_iota(jnp.int32, (REC_CHUNK, REC_CHUNK), 1)
    tril = ri >= ci
    nw = nw_ref[...].astype(F32)
    for g0 in range(0, n_chunks, REC_GROUP):
        group = range(g0, min(g0 + REC_GROUP, n_chunks))
        both = [lax.dot_general(qin_ref[chunk_rows(c), :], rhs_ref[c], (((1,), (1,)), ((), ())),
                                preferred_element_type=F32) for c in group]
        for c, b in zip(group, both):
            rows = chunk_rows(c)
            scores = jnp.where(tril, b[:, HEAD_DIM:], 0.0).astype(BF16)
            o = b[:, :HEAD_DIM] + jnp.dot(scores, v_ref[rows, :], preferred_element_type=F32)
            o = o * lax.rsqrt(jnp.mean(o * o, axis=-1, keepdims=True) + NORM_EPS)
            o = o * nw * _silu(g_ref[rows, :].astype(F32))
            o_ref[rows, :] = o.astype(o_ref.dtype)
        yield


def _mixer_kernel(layer, proj, cos, sin, lbl, nw, wo, mixed_o, wo_bf16,
                  kaug, vaug, qs, qin, rhs, upd, cum):
    wo_bf16[...] = wo[...].astype(BF16)
    streams = []
    for i in range(proj.shape[1]):
        lanes = slice(i * HEAD_DIM, (i + 1) * HEAD_DIM)
        kind = [proj.at[k, i] for k in range(N_KINDS)]
        streams.append(_attn_steps(kind[0], kind[1], kind[2], kind[3], cos, sin, mixed_o.at[0, i],
                                   kaug.at[i], vaug.at[i], qs.at[i]))
        streams.append(_rec_steps(layer, kind[4], kind[5], kind[6], kind[7], lbl.at[:, lanes],
                                  nw.at[:, lanes], mixed_o.at[1, i], qin.at[i], rhs.at[i], upd.at[i], cum.at[i]))
    stop = object()
    while streams:
        streams = [g for g in streams if next(g, stop) is not stop]


def _mixers(proj, lbl, nw_row, cos_t, sin_t, w_out, *, layer, batch, seq, n_heads):
    n = batch * seq
    mix, d = w_out.shape[1:]
    hg = proj.shape[2]
    hp = MIX_HEADS if hg % MIX_HEADS == 0 else 1
    n_steps = batch * (n_heads // hp)
    wo_rows = mix // n_steps
    assert wo_rows * n_steps == mix and wo_rows % 16 == 0
    tbl = pl.BlockSpec((seq, HEAD_DIM), lambda b, h: (0, 0))
    n_chunks = seq // REC_CHUNK
    return pl.pallas_call(
        functools.partial(_mixer_kernel, layer),
        grid=(batch, n_heads // hp),
        in_specs=[
            pl.BlockSpec((None, N_KINDS, hp, seq, HEAD_DIM),
                         lambda b, h: (h * hp // hg, 0, h % (hg // hp), b, 0)),
            tbl, tbl,
            pl.BlockSpec((lbl.shape[0], hp * HEAD_DIM), lambda b, h: (0, h)),
            pl.BlockSpec((1, hp * HEAD_DIM), lambda b, h: (0, h)),
            pl.BlockSpec((None, wo_rows, d), lambda b, h: (layer, b * (n_heads // hp) + h, 0)),
        ],
        out_specs=[pl.BlockSpec((2, hp, seq, HEAD_DIM), lambda b, h: (0, h, b, 0)),
                   pl.BlockSpec((wo_rows, d), lambda b, h: (b * (n_heads // hp) + h, 0))],
        out_shape=[jax.ShapeDtypeStruct((2, n_heads, n, HEAD_DIM), BF16),
                   jax.ShapeDtypeStruct((mix, d), BF16)],
        scratch_shapes=[
            pltpu.VMEM((hp, seq, 2 * HEAD_DIM), BF16),
            pltpu.VMEM((hp, seq, 2 * HEAD_DIM), BF16),
            pltpu.VMEM((hp, seq, HEAD_DIM), BF16),
            pltpu.VMEM((hp, seq, HEAD_DIM), BF16),
            pltpu.VMEM((hp, n_chunks, HEAD_DIM + REC_CHUNK, HEAD_DIM), BF16),
            pltpu.VMEM((hp, n_chunks, HEAD_DIM, HEAD_DIM), F32),
            pltpu.VMEM((hp, seq, HEAD_DIM), F32),
        ],
        compiler_params=_params(2),
        name="mixers",
    )(proj, cos_t, sin_t, lbl, nw_row, w_out)


def _out_kernel(final_norm, n_k, lhs_ref, w_ref, x_ref, nw_ref, o_ref, acc_ref):
    kk = pl.program_id(1)
    rs, cw = x_ref.shape
    rows = pl.ds(pl.multiple_of(pl.program_id(2) * rs, rs), rs)

    def partial_product():
        heads = [lhs_ref[c] for c in range(lhs_ref.shape[0])]
        return jnp.dot(jnp.concatenate(heads, axis=1), w_ref[...], preferred_element_type=F32)

    def add_residual(j):
        acc_ref[rows, j * cw:(j + 1) * cw] += x_ref[...]

    def finish():
        h = acc_ref[rows, :]
        if final_norm:
            ms = jnp.mean(h * h, axis=-1, keepdims=True)
            h = h * lax.rsqrt(ms + NORM_EPS) * nw_ref[...]
        o_ref[...] = h

    if n_k == 1:
        acc_ref[rows, :] = partial_product()
        add_residual(0)
        finish()
        return

    @pl.when(kk == 0)
    def _():
        acc_ref[rows, :] = partial_product()
        add_residual(0)

    @pl.when((kk > 0) & (kk < n_k - 1))
    def _():
        acc_ref[rows, :] += partial_product()

    for j in range(1, n_k - 1):
        @pl.when(kk == j)
        def _(j=j):
            add_residual(j)

    @pl.when(kk == n_k - 1)
    def _():
        acc_ref[rows, :] += partial_product()
        add_residual(n_k - 1)
        finish()


def _out_proj(mixed, w_bf16, x2d, nw_row, *, final_norm, bm=1024, rs=512, bk=1024):
    n, d = x2d.shape
    mix = mixed.shape[0] * HEAD_DIM
    bm = _pick_block(n, bm, 8)
    rs = _pick_block(bm, rs, 8)
    n_r = bm // rs
    bk = _pick_block(mix, bk, 128)
    n_k = mix // bk
    assert d % (n_k * 128) == 0

    def group(i, r):
        return i * n_r + r

    return pl.pallas_call(
        functools.partial(_out_kernel, final_norm, n_k),
        grid=(n // bm, n_k, n_r),
        in_specs=[
            pl.BlockSpec((bk // HEAD_DIM, rs, HEAD_DIM), lambda i, k, r: (k, group(i, r), 0)),
            pl.BlockSpec((bk, d), lambda i, k, r: (k, 0)),
            pl.BlockSpec((rs, d // n_k), lambda i, k, r: (group(i, r), k)),
            pl.BlockSpec((1, d), lambda i, k, r: (0, 0)),
        ],
        out_specs=pl.BlockSpec((rs, d), lambda i, k, r: (group(i, jnp.where(k == n_k - 1, r, 0)), 0)),
        out_shape=jax.ShapeDtypeStruct((n, d), F32),
        scratch_shapes=[pltpu.VMEM((bm, d), F32)],
        compiler_params=_params(3),
        name="out_proj",
    )(mixed, w_bf16, x2d, nw_row)


def _rope_tables(seq):
    half = ROPE_DIM // 2
    inv_freq = jnp.power(ROPE_THETA, -jnp.arange(half, dtype=F32) / half)
    ang = jnp.arange(seq, dtype=F32)[:, None] * inv_freq[None, :]
    cos, sin = jnp.cos(ang), jnp.sin(ang)
    rest = HEAD_DIM - ROPE_DIM
    cos_t = jnp.concatenate([cos, cos, jnp.ones((seq, rest), F32)], axis=1)
    sin_t = jnp.concatenate([-sin, sin, jnp.zeros((seq, rest), F32)], axis=1)
    return cos_t, sin_t


def kernel(x, norm_w, w_in, rec_lower_bound_logits, rec_out_norm_w, w_out, final_norm_w):
    batch, seq, d_model = x.shape
    depth = norm_w.shape[0]
    mix = w_out.shape[1]
    attn_w = mix // 2
    rec_w = mix - attn_w
    n_ah = attn_w // HEAD_DIM
    n_rh = rec_w // HEAD_DIM
    key_w = n_rh * HEAD_DIM
    assert seq % MOBA_BLOCK == 0 and seq % REC_CHUNK == 0
    assert n_ah == n_rh
    assert w_in.shape[2] == 4 * attn_w + 2 * key_w + 2 * rec_w

    cos_t, sin_t = _rope_tables(seq)
    h = x.reshape(batch * seq, d_model)
    for layer in range(depth):
        proj = _norm_proj(h, norm_w[layer].reshape(1, d_model), w_in, layer)
        rec_nw = rec_out_norm_w[layer].reshape(1, rec_w)
        mixed, w_out_bf16 = _mixers(proj, rec_lower_bound_logits, rec_nw, cos_t, sin_t, w_out,
                                    layer=layer, batch=batch, seq=seq, n_heads=n_ah)
        last = layer == depth - 1
        h = _out_proj(mixed.reshape(2 * n_ah, batch * seq, HEAD_DIM), w_out_bf16, h,
                      final_norm_w.reshape(1, d_model),
                      final_norm=last)
    return h.reshape(batch, seq, d_model)
```

```python
import functools

import jax
import jax.numpy as jnp
import numpy as np
from jax import lax
from jax.experimental import pallas as pl
from jax.experimental.pallas import tpu as pltpu

HEAD_DIM = 128
ROPE_THETA = 500000.0
ROPE_DIM = HEAD_DIM // 4
MOBA_BLOCK = 256
MOBA_TOPK = 3
REC_CHUNK = 64
REC_GROUP = 8
MIX_HEADS = 2
N_KINDS = 8
NORM_EPS = 1e-6
NEG_INF = -1e30

VMEM_LIMIT_BYTES = 60 * 1024 * 1024

F32 = jnp.float32
BF16 = jnp.bfloat16


def _params(n_grid_dims):
    return pltpu.CompilerParams(
        dimension_semantics=("arbitrary",) * n_grid_dims,
        vmem_limit_bytes=VMEM_LIMIT_BYTES,
    )


def _pick_block(dim, target, align):
    best = None
    for b in range(align, min(dim, target) + 1, align):
        if dim % b == 0:
            best = b
    assert best is not None, (dim, target, align)
    return best


def _sigmoid(x):
    return 0.5 * jnp.tanh(0.5 * x) + 0.5


def _silu(x):
    return x * _sigmoid(x)


def _norm_proj_kernel(x_ref, nw_ref, w_ref, o_ref, u_even, u_odd):
    i = pl.program_id(0)
    j = pl.program_id(1)
    strip = x_ref.shape[0]

    def normalize_into(u_dst):
        x = x_ref[...]
        ms = jnp.mean(x * x, axis=-1, keepdims=True)
        rows = pl.ds(pl.multiple_of(j * strip, strip), strip)
        u_dst[rows, :] = (x * lax.rsqrt(ms + NORM_EPS) * nw_ref[...]).astype(BF16)

    def project(u_src):
        acc = jnp.dot(u_src[...], w_ref[...].astype(BF16), preferred_element_type=F32)
        for c in range(o_ref.shape[0]):
            o_ref[c] = acc[:, c * HEAD_DIM:(c + 1) * HEAD_DIM].astype(o_ref.dtype)

    @pl.when(i == 0)
    def _():
        normalize_into(u_even)

    @pl.when((i > 0) & (i % 2 == 1))
    def _():
        project(u_even)
        normalize_into(u_odd)

    @pl.when((i > 0) & (i % 2 == 0))
    def _():
        project(u_odd)
        normalize_into(u_even)


def _norm_proj(x2d, nw_row, w, layer, *, bm=2048, bn=512):
    n, d = x2d.shape
    in_w = w.shape[2]
    kind_w = in_w // N_KINDS
    bm = _pick_block(n, bm, 8)
    bn = _pick_block(kind_w, bn, 128)
    n_groups = kind_w // bn
    n_i, n_j = n // bm, in_w // bn
    strip = bm // n_j
    assert strip * n_j == bm and strip % 8 == 0, (bm, n_j)
    last_strip = n // strip - 1

    def out_block(i, j):
        j = jnp.where(i == 0, 0, j)
        return (j % n_groups, j // n_groups, 0, jnp.maximum(i - 1, 0), 0)

    return pl.pallas_call(
        _norm_proj_kernel,
        grid=(n_i + 1, n_j),
        in_specs=[
            pl.BlockSpec((strip, d), lambda i, j: (jnp.minimum(i * n_j + j, last_strip), 0)),
            pl.BlockSpec((1, d), lambda i, j: (0, 0)),
            pl.BlockSpec((None, d, bn), lambda i, j: (layer, 0, jnp.where(i == 0, 0, j))),
        ],
        out_specs=pl.BlockSpec((None, None, bn // HEAD_DIM, bm, HEAD_DIM), out_block),
        out_shape=jax.ShapeDtypeStruct((n_groups, N_KINDS, bn // HEAD_DIM, n, HEAD_DIM), BF16),
        scratch_shapes=[pltpu.VMEM((bm, d), BF16), pltpu.VMEM((bm, d), BF16)],
        compiler_params=_params(2),
        name="norm_in_proj",
    )(x2d, nw_row, w)


def _attn_steps(q_ref, k_ref, v_ref, g_ref, cos_ref, sin_ref, o_ref, kaug_ref, vaug_ref, qs_ref):
    t = q_ref.shape[0]
    nb = t // MOBA_BLOCK
    nbp = max(16, -(-nb // 16) * 16)
    half = ROPE_DIM // 2

    src = lax.broadcasted_iota(jnp.int32, (HEAD_DIM, HEAD_DIM), 0)
    dst = lax.broadcasted_iota(jnp.int32, (HEAD_DIM, HEAD_DIM), 1)
    pick = ((dst < half) & (src == dst + half)) | ((dst >= half) & (dst < 2 * half) & (src == dst - half))
    pick = pick.astype(BF16)
    lane = lax.broadcasted_iota(jnp.int32, (MOBA_BLOCK, HEAD_DIM), 1)

    k_means = []
    for blk in range(nb):
        rows = slice(blk * MOBA_BLOCK, (blk + 1) * MOBA_BLOCK)
        cos = cos_ref[rows, :]
        sin = sin_ref[rows, :]

        def rope(x_bf16):
            partner = jnp.dot(x_bf16, pick, preferred_element_type=F32)
            return x_bf16.astype(F32) * cos + partner * sin

        k = rope(k_ref[rows, :])
        kaug_ref[rows, :HEAD_DIM] = k.astype(BF16)
        kaug_ref[rows, HEAD_DIM:] = (lane == blk).astype(BF16)
        k_means.append(jnp.mean(k, axis=0, keepdims=True))
        qs_ref[rows, :] = (rope(q_ref[rows, :]) * float(HEAD_DIM ** -0.5 * np.log2(np.e))).astype(BF16)
        vaug_ref[rows, :HEAD_DIM] = v_ref[rows, :]
        vaug_ref[rows, HEAD_DIM:] = jnp.ones((MOBA_BLOCK, HEAD_DIM), BF16)

    k_mean = jnp.concatenate(k_means + [jnp.zeros((nbp - nb, HEAD_DIM), F32)], axis=0).astype(BF16)
    gate_t = lax.dot_general(k_mean, qs_ref[...], (((1,), (1,)), ((), ())),
                             preferred_element_type=F32)

    row_blk = lax.broadcasted_iota(jnp.int32, (nbp, MOBA_BLOCK), 0)
    q_idx = lax.broadcasted_iota(jnp.int32, (MOBA_BLOCK, MOBA_BLOCK), 0)
    k_idx = lax.broadcasted_iota(jnp.int32, (MOBA_BLOCK, MOBA_BLOCK), 1)
    causal = k_idx <= q_idx

    def masked_scores(qb):
        rows = slice(qb * MOBA_BLOCK, (qb + 1) * MOBA_BLOCK)
        nk = (qb + 1) * MOBA_BLOCK
        q_blk = qs_ref[rows, :]
        if qb > MOBA_TOPK:
            g = gate_t[:, rows]
            rank = jnp.zeros((nbp, MOBA_BLOCK), F32)
            for m in range(qb):
                gm = g[m:m + 1, :]
                beats = (gm > g) | ((gm == g) & (m < row_blk))
                rank = rank + beats.astype(F32)
            keep = ((rank < MOBA_TOPK) & (row_blk < qb)) | (row_blk == qb)
            bias_t = jnp.where(keep, 0.0, NEG_INF)
            bias_t = jnp.concatenate([bias_t, jnp.zeros((HEAD_DIM - nbp, MOBA_BLOCK), F32)], axis=0)
            q_aug = jnp.concatenate([q_blk, bias_t.T.astype(BF16)], axis=1)
            s = lax.dot_general(q_aug, kaug_ref[:nk, :], (((1,), (1,)), ((), ())),
                                preferred_element_type=F32)
        else:
            s = lax.dot_general(q_blk, kaug_ref[:nk, :HEAD_DIM], (((1,), (1,)), ((), ())),
                                preferred_element_type=F32)
        s_own = jnp.where(causal, s[:, qb * MOBA_BLOCK:], NEG_INF)
        if qb > 0:
            return jnp.concatenate([s[:, :qb * MOBA_BLOCK], s_own], axis=1)
        return s_own

    def finish(qb, p):
        rows = slice(qb * MOBA_BLOCK, (qb + 1) * MOBA_BLOCK)
        oa = jnp.dot(p, vaug_ref[:(qb + 1) * MOBA_BLOCK, :], preferred_element_type=F32)
        o = oa[:, :HEAD_DIM] / oa[:, HEAD_DIM:]
        o_ref[rows, :] = (o * _silu(g_ref[rows, :].astype(F32))).astype(o_ref.dtype)

    s_cur = masked_scores(0)
    p_prev = None
    yield
    for qb in range(nb):
        s_next = masked_scores(qb + 1) if qb + 1 < nb else None
        if p_prev is not None:
            finish(qb - 1, p_prev)
        m_row = jnp.max(s_cur, axis=-1, keepdims=True)
        p_prev = jnp.exp2(s_cur - m_row).astype(BF16)
        s_cur = s_next
        yield
    finish(nb - 1, p_prev)
    yield


def _rec_steps(layer, q_ref, z_ref, v_ref, g_ref, lbl_ref, nw_ref, o_ref,
               qin_ref, rhs_ref, upd_ref, cum_ref):
    t = q_ref.shape[0]
    n_chunks = t // REC_CHUNK

    logits = lbl_ref[...].astype(F32)
    e = jnp.exp(logits - jnp.max(logits, axis=0, keepdims=True))
    probs = e / jnp.sum(e, axis=0, keepdims=True)
    lb = jnp.sum(probs[:layer + 1], axis=0, keepdims=True)

    def chunk_rows(c):
        return slice(c * REC_CHUNK, (c + 1) * REC_CHUNK)

    mid = REC_CHUNK // 2 - 1
    tri = (lax.broadcasted_iota(jnp.int32, (REC_CHUNK, 2 * REC_CHUNK), 0)
           >= lax.broadcasted_iota(jnp.int32, (REC_CHUNK, 2 * REC_CHUNK), 1) % REC_CHUNK).astype(BF16)
    for c in range(n_chunks):
        rows = chunk_rows(c)
        half_t = 0.5 * jnp.tanh(0.5 * z_ref[rows, :].astype(F32))
        log2_f = jnp.log2(lb + (1.0 - lb) * (0.5 + half_t))
        k_r = (1.0 - lb) * (0.5 - half_t)
        q_r = _silu(q_ref[rows, :].astype(F32)) * (HEAD_DIM ** -0.5)
        high = log2_f.astype(BF16)
        low = (log2_f - high.astype(F32)).astype(BF16)
        cum = jnp.dot(tri, jnp.concatenate([high, low], axis=0), preferred_element_type=F32)
        cum_ref[rows, :] = cum
        rel = cum - cum[mid:mid + 1, :]
        qin_ref[rows, :] = (q_r * jnp.exp2(rel)).astype(BF16)
        rhs_ref[c, HEAD_DIM:, :] = (k_r * jnp.exp2(-rel)).astype(BF16)
        if (c + 1) % REC_GROUP == 0:
            yield

    for c in range(n_chunks):
        upd_ref[c] = lax.dot_general(v_ref[chunk_rows(c), :], rhs_ref[c, HEAD_DIM:, :],
                                     (((0,), (0,)), ((), ())), preferred_element_type=F32)
        if (c + 1) % REC_GROUP == 0:
            yield

    scaled_t = jnp.zeros((HEAD_DIM, HEAD_DIM), F32)
    for c in range(n_chunks):
        rhs_ref[c, :HEAD_DIM, :] = scaled_t.astype(BF16)
        if c + 1 < n_chunks:
            c_mid = cum_ref[c * REC_CHUNK + mid:c * REC_CHUNK + mid + 1, :]
            c_last = cum_ref[(c + 1) * REC_CHUNK - 1:(c + 1) * REC_CHUNK, :]
            next_mid = cum_ref[(c + 1) * REC_CHUNK + mid:(c + 1) * REC_CHUNK + mid + 1, :]
            scaled_t = (scaled_t + upd_ref[c]) * jnp.exp2(c_last - c_mid + next_mid)
        if (c + 1) % REC_GROUP == 0:
            yield

    ri = lax.broadcasted_iota(jnp.int32, (REC_CHUNK, REC_CHUNK), 0)
    ci = lax.broadcasted_iota(jnp.int32, (REC_CHUNK, REC_CHUNK), 1)
    tril = ri >= ci
    nw = nw_ref[...].astype(F32)
    for g0 in range(0, n_chunks, REC_GROUP):
        group = range(g0, min(g0 + REC_GROUP, n_chunks))
        both = [lax.dot_general(qin_ref[chunk_rows(c), :], rhs_ref[c], (((1,), (1,)), ((), ())),
                                preferred_element_type=F32) for c in group]
        for c, b in zip(group, both):
            rows = chunk_rows(c)
            scores = jnp.where(tril, b[:, HEAD_DIM:], 0.0).astype(BF16)
            o = b[:, :HEAD_DIM] + jnp.dot(scores, v_ref[rows, :], preferred_element_type=F32)
            o = o * lax.rsqrt(jnp.mean(o * o, axis=-1, keepdims=True) + NORM_EPS)
            o = o * nw * _silu(g_ref[rows, :].astype(F32))
            o_ref[rows, :] = o.astype(o_ref.dtype)
        yield


def _mixer_kernel(layer, proj, cos, sin, lbl, nw, wo, mixed_o, wo_bf16,
                  kaug, vaug, qs, qin, rhs, upd, cum):
    wo_bf16[...] = wo[...].astype(BF16)
    streams = []
    for i in range(proj.shape[1]):
        lanes = slice(i * HEAD_DIM, (i + 1) * HEAD_DIM)
        kind = [proj.at[k, i] for k in range(N_KINDS)]
        streams.append(_attn_steps(kind[0], kind[1], kind[2], kind[3], cos, sin, mixed_o.at[0, i],
                                   kaug.at[i], vaug.at[i], qs.at[i]))
        streams.append(_rec_steps(layer, kind[4], kind[5], kind[6], kind[7], lbl.at[:, lanes],
                                  nw.at[:, lanes], mixed_o.at[1, i], qin.at[i], rhs.at[i], upd.at[i], cum.at[i]))
    stop = object()
    while streams:
        streams = [g for g in streams if next(g, stop) is not stop]


def _mixers(proj, lbl, nw_row, cos_t, sin_t, w_out, *, layer, batch, seq, n_heads):
    n = batch * seq
    mix, d = w_out.shape[1:]
    hg = proj.shape[2]
    hp = MIX_HEADS if hg % MIX_HEADS == 0 else 1
    n_steps = batch * (n_heads // hp)
    wo_rows = mix // n_steps
    assert wo_rows * n_steps == mix and wo_rows % 16 == 0
    tbl = pl.BlockSpec((seq, HEAD_DIM), lambda b, h: (0, 0))
    n_chunks = seq // REC_CHUNK
    return pl.pallas_call(
        functools.partial(_mixer_kernel, layer),
        grid=(batch, n_heads // hp),
        in_specs=[
            pl.BlockSpec((None, N_KINDS, hp, seq, HEAD_DIM),
                         lambda b, h: (h * hp // hg, 0, h % (hg // hp), b, 0)),
            tbl, tbl,
            pl.BlockSpec((lbl.shape[0], hp * HEAD_DIM), lambda b, h: (0, h)),
            pl.BlockSpec((1, hp * HEAD_DIM), lambda b, h: (0, h)),
            pl.BlockSpec((None, wo_rows, d), lambda b, h: (layer, b * (n_heads // hp) + h, 0)),
        ],
        out_specs=[pl.BlockSpec((2, hp, seq, HEAD_DIM), lambda b, h: (0, h, b, 0)),
                   pl.BlockSpec((wo_rows, d), lambda b, h: (b * (n_heads // hp) + h, 0))],
        out_shape=[jax.ShapeDtypeStruct((2, n_heads, n, HEAD_DIM), BF16),
                   jax.ShapeDtypeStruct((mix, d), BF16)],
        scratch_shapes=[
            pltpu.VMEM((hp, seq, 2 * HEAD_DIM), BF16),
            pltpu.VMEM((hp, seq, 2 * HEAD_DIM), BF16),
            pltpu.VMEM((hp, seq, HEAD_DIM), BF16),
            pltpu.VMEM((hp, seq, HEAD_DIM), BF16),
            pltpu.VMEM((hp, n_chunks, HEAD_DIM + REC_CHUNK, HEAD_DIM), BF16),
            pltpu.VMEM((hp, n_chunks, HEAD_DIM, HEAD_DIM), F32),
            pltpu.VMEM((hp, seq, HEAD_DIM), F32),
        ],
        compiler_params=_params(2),
        name="mixers",
    )(proj, cos_t, sin_t, lbl, nw_row, w_out)


def _out_kernel(final_norm, n_k, lhs_ref, w_ref, x_ref, nw_ref, o_ref, acc_ref):
    kk = pl.program_id(1)
    rs, cw = x_ref.shape
    rows = pl.ds(pl.multiple_of(pl.program_id(2) * rs, rs), rs)

    def partial_product():
        heads = [lhs_ref[c] for c in range(lhs_ref.shape[0])]
        return jnp.dot(jnp.concatenate(heads, axis=1), w_ref[...], preferred_element_type=F32)

    def add_residual(j):
        acc_ref[rows, j * cw:(j + 1) * cw] += x_ref[...]

    def finish():
        h = acc_ref[rows, :]
        if final_norm:
            ms = jnp.mean(h * h, axis=-1, keepdims=True)
            h = h * lax.rsqrt(ms + NORM_EPS) * nw_ref[...]
        o_ref[...] = h

    if n_k == 1:
        acc_ref[rows, :] = partial_product()
        add_residual(0)
        finish()
        return

    @pl.when(kk == 0)
    def _():
        acc_ref[rows, :] = partial_product()
        add_residual(0)

    @pl.when((kk > 0) & (kk < n_k - 1))
    def _():
        acc_ref[rows, :] += partial_product()

    for j in range(1, n_k - 1):
        @pl.when(kk == j)
        def _(j=j):
            add_residual(j)

    @pl.when(kk == n_k - 1)
    def _():
        acc_ref[rows, :] += partial_product()
        add_residual(n_k - 1)
        finish()


def _out_proj(mixed, w_bf16, x2d, nw_row, *, final_norm, bm=1024, rs=512, bk=1024):
    n, d = x2d.shape
    mix = mixed.shape[0] * HEAD_DIM
    bm = _pick_block(n, bm, 8)
    rs = _pick_block(bm, rs, 8)
    n_r = bm // rs
    bk = _pick_block(mix, bk, 128)
    n_k = mix // bk
    assert d % (n_k * 128) == 0

    def group(i, r):
        return i * n_r + r

    return pl.pallas_call(
        functools.partial(_out_kernel, final_norm, n_k),
        grid=(n // bm, n_k, n_r),
        in_specs=[
            pl.BlockSpec((bk // HEAD_DIM, rs, HEAD_DIM), lambda i, k, r: (k, group(i, r), 0)),
            pl.BlockSpec((bk, d), lambda i, k, r: (k, 0)),
            pl.BlockSpec((rs, d // n_k), lambda i, k, r: (group(i, r), k)),
            pl.BlockSpec((1, d), lambda i, k, r: (0, 0)),
        ],
        out_specs=pl.BlockSpec((rs, d), lambda i, k, r: (group(i, jnp.where(k == n_k - 1, r, 0)), 0)),
        out_shape=jax.ShapeDtypeStruct((n, d), F32),
        scratch_shapes=[pltpu.VMEM((bm, d), F32)],
        compiler_params=_params(3),
        name="out_proj",
    )(mixed, w_bf16, x2d, nw_row)


def _rope_tables(seq):
    half = ROPE_DIM // 2
    inv_freq = jnp.power(ROPE_THETA, -jnp.arange(half, dtype=F32) / half)
    ang = jnp.arange(seq, dtype=F32)[:, None] * inv_freq[None, :]
    cos, sin = jnp.cos(ang), jnp.sin(ang)
    rest = HEAD_DIM - ROPE_DIM
    cos_t = jnp.concatenate([cos, cos, jnp.ones((seq, rest), F32)], axis=1)
    sin_t = jnp.concatenate([-sin, sin, jnp.zeros((seq, rest), F32)], axis=1)
    return cos_t, sin_t


def kernel(x, norm_w, w_in, rec_lower_bound_logits, rec_out_norm_w, w_out, final_norm_w):
    batch, seq, d_model = x.shape
    depth = norm_w.shape[0]
    mix = w_out.shape[1]
    attn_w = mix // 2
    rec_w = mix - attn_w
    n_ah = attn_w // HEAD_DIM
    n_rh = rec_w // HEAD_DIM
    key_w = n_rh * HEAD_DIM
    assert seq % MOBA_BLOCK == 0 and seq % REC_CHUNK == 0
    assert n_ah == n_rh
    assert w_in.shape[2] == 4 * attn_w + 2 * key_w + 2 * rec_w

    cos_t, sin_t = _rope_tables(seq)
    h = x.reshape(batch * seq, d_model)
    for layer in range(depth):
        proj = _norm_proj(h, norm_w[layer].reshape(1, d_model), w_in, layer)
        rec_nw = rec_out_norm_w[layer].reshape(1, rec_w)
        mixed, w_out_bf16 = _mixers(proj, rec_lower_bound_logits, rec_nw, cos_t, sin_t, w_out,
                                    layer=layer, batch=batch, seq=seq, n_heads=n_ah)
        last = layer == depth - 1
        h = _out_proj(mixed.reshape(2 * n_ah, batch * seq, HEAD_DIM), w_out_bf16, h,
                      final_norm_w.reshape(1, d_model),
                      final_norm=last)
    return h.reshape(batch, seq, d_model)
```
